```python
import math
import jax, jax.numpy as jnp
from jax import lax
import numpy as np

D_MODEL = 1024
BATCH = 8
SEQ = 8192
DEPTH = 1
DEC_BATCH = 8
DEC_SEQ = 64
PAST_LEN = 2048

CHUNK = 64
D_MIX = D_MODEL
SSM_WIDTH = D_MIX // 2
SSM_GROUP = 16
SSM_GROUPS = SSM_WIDTH // SSM_GROUP
SSM_STATE = 64
SSM_BLOCK = 128
ATTN_WIDTH = D_MIX - SSM_WIDTH
HEAD_DIM = 64
N_HEADS = ATTN_WIDTH // HEAD_DIM
Q_BLOCK = 128
ATTN_SCALE = HEAD_DIM ** -0.5
NORM_EPS = 1e-6
IN_WIDTH = 2 * SSM_WIDTH + 4 * ATTN_WIDTH + N_HEADS
IN_SPLITS = [SSM_WIDTH, 2 * SSM_WIDTH, 2 * SSM_WIDTH + ATTN_WIDTH, 2 * SSM_WIDTH + 2 * ATTN_WIDTH,
             2 * SSM_WIDTH + 3 * ATTN_WIDTH, 2 * SSM_WIDTH + 4 * ATTN_WIDTH]

kernel_name = 'hymba_s5_fox_stream_step'


def _rmsnorm(x, g):
    xf = x.astype(jnp.float32)
    y = xf * lax.rsqrt(jnp.mean(xf * xf, axis=-1, keepdims=True) + NORM_EPS)
    return (y * g.astype(jnp.float32)).astype(x.dtype)


def _zoh(log_dt, a_re, a_im, b_re, b_im):
    f32 = jnp.float32
    dt = jnp.exp(log_dt.astype(f32))[:, None]
    a_re = a_re.astype(f32)
    a_im = a_im.astype(f32)
    mag = jnp.exp(a_re * dt)
    ang = a_im * dt
    abar_re = mag * jnp.cos(ang)
    abar_im = mag * jnp.sin(ang)
    den = a_re * a_re + a_im * a_im
    n_re = abar_re - 1.0
    n_im = abar_im
    q_re = (n_re * a_re + n_im * a_im) / den
    q_im = (n_im * a_re - n_re * a_im) / den
    b_re = b_re.astype(f32)
    b_im = b_im.astype(f32)
    bbar_re = q_re[..., None] * b_re - q_im[..., None] * b_im
    bbar_im = q_re[..., None] * b_im + q_im[..., None] * b_re
    return abar_re, abar_im, bbar_re, bbar_im


def _complex_combine(e1, e2):
    a1r, a1i, b1r, b1i = e1
    a2r, a2i, b2r, b2i = e2
    return (a2r * a1r - a2i * a1i, a2r * a1i + a2i * a1r,
            a2r * b1r - a2i * b1i + b2r, a2r * b1i + a2i * b1r + b2i)


def _s5_segment(u, h_re, h_im, abar_re, abar_im, bbar_re, bbar_im, c_re, c_im, d_skip):
    bu_re = jnp.einsum('btgp,gnp->btgn', u, bbar_re)
    bu_im = jnp.einsum('btgp,gnp->btgn', u, bbar_im)
    bu_re = bu_re.at[:, 0].add(abar_re * h_re - abar_im * h_im)
    bu_im = bu_im.at[:, 0].add(abar_re * h_im + abar_im * h_re)
    a_re = jnp.broadcast_to(abar_re, bu_re.shape)
    a_im = jnp.broadcast_to(abar_im, bu_im.shape)
    _, _, x_re, x_im = lax.associative_scan(_complex_combine, (a_re, a_im, bu_re, bu_im), axis=1)
    y = (jnp.einsum('btgn,gpn->btgp', x_re, c_re) - jnp.einsum('btgn,gpn->btgp', x_im, c_im)
         + d_skip * u)
    return y, x_re[:, -1], x_im[:, -1]


def _fox_attend(q, cum_q, pos_q, k, v, cum_k, pos_k):
    s = jnp.einsum('bqhd,bkhd->bhqk', q, k).astype(jnp.float32) * ATTN_SCALE
    s = s + jnp.transpose(cum_q, (0, 2, 1))[:, :, :, None] - jnp.transpose(cum_k, (0, 2, 1))[:, :, None, :]
    mask = pos_k[None, :] <= pos_q[:, None]
    s = jnp.where(mask[None, None], s, -jnp.inf)
    p = jax.nn.softmax(s, axis=-1)
    return jnp.einsum('bhqk,bkhd->bqhd', p.astype(v.dtype), v)


def _layer(x, c, h_re, h_im, past, w_ada, b_ada, norm_g, w_in, b_f, q_norm_g, k_norm_g,
           ssm_log_dt, ssm_a_re, ssm_a_im, ssm_b_re, ssm_b_im, ssm_c_re, ssm_c_im, ssm_d,
           w_glu, b_glu, w_out):
    f32 = jnp.float32
    n_b, n_t, _ = x.shape
    mod = jax.nn.silu(c) @ w_ada + b_ada
    shift, scale, gate = jnp.split(mod, 3, axis=-1)
    h = _rmsnorm(x, norm_g) * (1.0 + scale[:, None, :]) + shift[:, None, :]
    proj = h @ w_in
    u, z_s, q, k, v, z_a, f_logit = jnp.split(proj, IN_SPLITS, axis=-1)

    abar_re, abar_im, bbar_re, bbar_im = _zoh(ssm_log_dt, ssm_a_re, ssm_a_im, ssm_b_re, ssm_b_im)
    c_re = ssm_c_re.astype(f32)
    c_im = ssm_c_im.astype(f32)
    d_skip = ssm_d.astype(f32)
    ug = u.astype(f32).reshape(n_b, n_t, SSM_GROUPS, SSM_GROUP)
    if h_re is None:
        n_blk = n_t // SSM_BLOCK
        ub = ug.reshape(n_b, n_blk, SSM_BLOCK, SSM_GROUPS, SSM_GROUP).transpose(1, 0, 2, 3, 4)
        h0 = jnp.zeros((n_b, SSM_GROUPS, SSM_STATE), f32)

        def step(carry, u_blk):
            y_blk, hr, hi = _s5_segment(u_blk, carry[0], carry[1], abar_re, abar_im,
                                        bbar_re, bbar_im, c_re, c_im, d_skip)
            return (hr, hi), y_blk

        (h_re_new, h_im_new), ys = lax.scan(step, (h0, h0), ub)
        y_ssm = ys.transpose(1, 0, 2, 3, 4).reshape(n_b, n_t, SSM_WIDTH)
    else:
        y_ssm, h_re_new, h_im_new = _s5_segment(ug, h_re.astype(f32), h_im.astype(f32), abar_re, abar_im,
                                                bbar_re, bbar_im, c_re, c_im, d_skip)
        y_ssm = y_ssm.reshape(n_b, n_t, SSM_WIDTH)
    y_ssm = jax.nn.gelu(y_ssm)
    y_ssm = y_ssm * jax.nn.sigmoid(y_ssm @ w_glu.astype(f32) + b_glu.astype(f32))
    y_ssm = (y_ssm * jax.nn.silu(z_s.astype(f32))).astype(x.dtype)

    q = _rmsnorm(q.reshape(n_b, n_t, N_HEADS, HEAD_DIM), q_norm_g)
    k = _rmsnorm(k.reshape(n_b, n_t, N_HEADS, HEAD_DIM), k_norm_g)
    v = v.reshape(n_b, n_t, N_HEADS, HEAD_DIM)
    logf = jax.nn.log_sigmoid((f_logit + b_f).astype(f32))
    if past is None:
        cum = jnp.cumsum(logf, axis=1)
        pos = jnp.arange(n_t)
        n_qb = n_t // Q_BLOCK
        qb = q.reshape(n_b, n_qb, Q_BLOCK, N_HEADS, HEAD_DIM).transpose(1, 0, 2, 3, 4)
        cb = cum.reshape(n_b, n_qb, Q_BLOCK, N_HEADS).transpose(1, 0, 2, 3)
        pb = pos.reshape(n_qb, Q_BLOCK)
        ob = lax.map(lambda blk: _fox_attend(blk[0], blk[1], blk[2], k, v, cum, pos), (qb, cb, pb))
        o = ob.transpose(1, 0, 2, 3, 4).reshape(n_b, n_t, ATTN_WIDTH)
    else:
        ck, cv, clogf = past
        n_past = ck.shape[1]
        k_all = jnp.concatenate([ck.astype(k.dtype), k], axis=1)
        v_all = jnp.concatenate([cv.astype(v.dtype), v], axis=1)
        cum = jnp.cumsum(jnp.concatenate([clogf.astype(f32), logf], axis=1), axis=1)
        pos_k = jnp.arange(n_past + n_t)
        o = _fox_attend(q, cum[:, n_past:], pos_k[n_past:], k_all, v_all, cum, pos_k)
        o = o.reshape(n_b, n_t, ATTN_WIDTH)
    y_att = (o.astype(f32) * jax.nn.silu(z_a.astype(f32))).astype(x.dtype)

    mixed = jnp.concatenate([y_ssm, y_att], axis=-1)
    y = x + gate[:, None, :] * (mixed @ w_out)
    return y, k, v, logf, h_re_new, h_im_new


def setup_inputs(seed: int = 0) -> dict:
    key = jax.random.key(seed)
    ks = jax.random.split(key, 32)
    f32 = jnp.float32

    def nrm(k, shape, s):
        return s * jax.random.normal(k, shape, f32)

    G, N, P = SSM_GROUPS, SSM_STATE, SSM_GROUP
    x_prompt = nrm(ks[0], (BATCH, SEQ, D_MODEL), 1.0)
    x_sample = nrm(ks[1], (DEC_BATCH, DEC_SEQ, D_MODEL), 1.0)
    cache_k = nrm(ks[2], (DEPTH, DEC_BATCH, PAST_LEN, N_HEADS, HEAD_DIM), 1.0)
    cache_v = nrm(ks[3], (DEPTH, DEC_BATCH, PAST_LEN, N_HEADS, HEAD_DIM), 1.0)
    cache_logf = jax.nn.log_sigmoid(2.0 + nrm(ks[4], (DEPTH, DEC_BATCH, PAST_LEN, N_HEADS), 0.5))
    state_ssm_re = nrm(ks[5], (DEPTH, DEC_BATCH, G, N), 0.1)
    state_ssm_im = nrm(ks[6], (DEPTH, DEC_BATCH, G, N), 0.1)
    c_prompt = nrm(ks[7], (BATCH, D_MODEL), 1.0)
    c_sample = nrm(ks[8], (DEC_BATCH, D_MODEL), 1.0)
    w_ada = nrm(ks[9], (DEPTH, D_MODEL, 3 * D_MODEL), 0.5 * D_MODEL ** -0.5)
    b_ada = nrm(ks[10], (DEPTH, 3 * D_MODEL), 0.02)
    norm_g = 1.0 + nrm(ks[11], (DEPTH, D_MODEL), 0.02)
    w_in = nrm(ks[12], (DEPTH, D_MODEL, IN_WIDTH), D_MODEL ** -0.5)
    b_f = jax.random.uniform(ks[13], (DEPTH, N_HEADS), f32, 1.0, 3.0)
    q_norm_g = 1.0 + nrm(ks[14], (DEPTH, HEAD_DIM), 0.02)
    k_norm_g = 1.0 + nrm(ks[15], (DEPTH, HEAD_DIM), 0.02)
    ssm_log_dt = jax.random.uniform(ks[16], (DEPTH, G), f32, math.log(1e-3), math.log(1e-1))
    ssm_a_re = -0.5 + nrm(ks[17], (DEPTH, G, N), 0.01)
    ssm_a_im = math.pi * jnp.arange(N, dtype=f32) + nrm(ks[18], (DEPTH, G, N), 0.01)
    ssm_b_re = nrm(ks[19], (DEPTH, G, N, P), (2 * P) ** -0.5)
    ssm_b_im = nrm(ks[20], (DEPTH, G, N, P), (2 * P) ** -0.5)
    ssm_c_re = nrm(ks[21], (DEPTH, G, P, N), N ** -0.5)
    ssm_c_im = nrm(ks[22], (DEPTH, G, P, N), N ** -0.5)
    ssm_d = nrm(ks[23], (DEPTH, G, P), 1.0)
    w_glu = nrm(ks[24], (DEPTH, SSM_WIDTH, SSM_WIDTH), SSM_WIDTH ** -0.5)
    b_glu = nrm(ks[25], (DEPTH, SSM_WIDTH), 0.02)
    w_out = nrm(ks[26], (DEPTH, D_MIX, D_MODEL), D_MIX ** -0.5)
    return {'x_prompt': x_prompt, 'x_sample': x_sample, 'cache_k': cache_k, 'cache_v': cache_v,
            'cache_logf': cache_logf, 'state_ssm_re': state_ssm_re, 'state_ssm_im': state_ssm_im,
            'c_prompt': c_prompt, 'c_sample': c_sample, 'w_ada': w_ada, 'b_ada': b_ada, 'norm_g': norm_g,
            'w_in': w_in, 'b_f': b_f, 'q_norm_g': q_norm_g, 'k_norm_g': k_norm_g, 'ssm_log_dt': ssm_log_dt,
            'ssm_a_re': ssm_a_re, 'ssm_a_im': ssm_a_im, 'ssm_b_re': ssm_b_re, 'ssm_b_im': ssm_b_im,
            'ssm_c_re': ssm_c_re, 'ssm_c_im': ssm_c_im, 'ssm_d': ssm_d, 'w_glu': w_glu, 'b_glu': b_glu,
            'w_out': w_out}


def reference(x_prompt, x_sample, cache_k, cache_v, cache_logf, state_ssm_re, state_ssm_im,
              c_prompt, c_sample, w_ada, b_ada, norm_g, w_in, b_f, q_norm_g, k_norm_g,
              ssm_log_dt, ssm_a_re, ssm_a_im, ssm_b_re, ssm_b_im, ssm_c_re, ssm_c_im, ssm_d,
              w_glu, b_glu, w_out):
    xp = x_prompt
    xs = x_sample
    kp, vp, fp, rp, ip = [], [], [], [], []
    ks_, vs_, fs_, rs_, is_ = [], [], [], [], []
    for l in range(DEPTH):
        lw = (w_ada[l], b_ada[l], norm_g[l], w_in[l], b_f[l], q_norm_g[l], k_norm_g[l],
              ssm_log_dt[l], ssm_a_re[l], ssm_a_im[l], ssm_b_re[l], ssm_b_im[l],
              ssm_c_re[l], ssm_c_im[l], ssm_d[l], w_glu[l], b_glu[l], w_out[l])
        xp, k1, v1, f1, r1, i1 = _layer(xp, c_prompt, None, None, None, *lw)
        xs, k2, v2, f2, r2, i2 = _layer(xs, c_sample, state_ssm_re[l], state_ssm_im[l],
                                        (cache_k[l], cache_v[l], cache_logf[l]), *lw)
        kp.append(k1); vp.append(v1); fp.append(f1); rp.append(r1); ip.append(i1)
        ks_.append(k2); vs_.append(v2); fs_.append(f2); rs_.append(r2); is_.append(i2)
    return (xp, xs, jnp.stack(kp), jnp.stack(vp), jnp.stack(fp), jnp.stack(rp), jnp.stack(ip),
            jnp.stack(ks_), jnp.stack(vs_), jnp.stack(fs_), jnp.stack(rs_), jnp.stack(is_))
```

```python
import functools

import numpy as np
import jax
import jax.numpy as jnp
from jax import lax
from jax.experimental import pallas as pl
from jax.experimental.pallas import tpu as pltpu

F32 = jnp.float32
BF16 = jnp.bfloat16

N_HEADS = 8
HEAD_DIM = 64
ATTN_WIDTH = N_HEADS * HEAD_DIM
SSM_GROUPS = 32
SSM_GROUP = 16
SSM_STATE = 64
SSM_WIDTH = SSM_GROUPS * SSM_GROUP
STATE_LANES = 2 * SSM_GROUPS * SSM_STATE
HALF_LANES = STATE_LANES // 2
HALF_CH = SSM_WIDTH // 2
ATTN_SCALE = HEAD_DIM ** -0.5
NORM_EPS = 1e-6
LANES = 128
SUBLANES = 8
NEG_BIG = -1e30
GELU_C = float(np.float32(np.sqrt(2.0 / np.pi)))
VMEM_LIMIT = 56 * 1024 * 1024


def _cparams(sem):
    return pltpu.CompilerParams(dimension_semantics=sem, vmem_limit_bytes=VMEM_LIMIT)


def _sigmoid(x):
    return 1.0 / (1.0 + jnp.exp(-x))


def _silu(x):
    return x * _sigmoid(x)


def _mod_kernel(c_ref, w_ref, b_ref, o_ref):
    c = c_ref[...]
    o_ref[...] = jnp.dot(_silu(c), w_ref[...], preferred_element_type=F32) + b_ref[...]


def _mod(c, w_ada, b_ada):
    n, d = c.shape
    n_out = w_ada.shape[1]
    blk = 1024
    return pl.pallas_call(
        _mod_kernel,
        grid=(n_out // blk,),
        in_specs=[pl.BlockSpec((n, d), lambda j: (0, 0)),
                  pl.BlockSpec((d, blk), lambda j: (0, j)),
                  pl.BlockSpec((1, blk), lambda j: (0, j))],
        out_specs=pl.BlockSpec((n, blk), lambda j: (0, j)),
        out_shape=jax.ShapeDtypeStruct((n, n_out), F32),
        compiler_params=_cparams(("arbitrary",)),
        name="mod",
    )(c, w_ada, b_ada.reshape(1, n_out))


def _zoh_kernel(ldt_ref, are_ref, aim_ref, bre_ref, bim_ref,
                abre_ref, abim_ref, bbre_ref, bbim_ref):
    dt = jnp.exp(ldt_ref[...])
    a_re = are_ref[...]
    a_im = aim_ref[...]
    mag = jnp.exp(a_re * dt)
    ang = a_im * dt
    abar_re = mag * jnp.cos(ang)
    abar_im = mag * jnp.sin(ang)
    den = a_re * a_re + a_im * a_im
    n_re = abar_re - 1.0
    n_im = abar_im
    q_re = (n_re * a_re + n_im * a_im) / den
    q_im = (n_im * a_re - n_re * a_im) / den
    b_re = bre_ref[...]
    b_im = bim_ref[...]
    abre_ref[...] = abar_re
    abim_ref[...] = abar_im
    bbre_ref[...] = q_re * b_re - q_im * b_im
    bbim_ref[...] = q_re * b_im + q_im * b_re


def _zoh(log_dt, a_re, a_im, b_re, b_im):
    g, n, p = b_re.shape
    rows = g * n
    ldt = jnp.broadcast_to(log_dt[:, None], (g, n)).reshape(rows, 1)
    outs = pl.pallas_call(
        _zoh_kernel,
        out_shape=[jax.ShapeDtypeStruct((rows, 1), F32), jax.ShapeDtypeStruct((rows, 1), F32),
                   jax.ShapeDtypeStruct((rows, p), F32), jax.ShapeDtypeStruct((rows, p), F32)],
        name="zoh",
    )(ldt, a_re.reshape(rows, 1), a_im.reshape(rows, 1), b_re.reshape(rows, p), b_im.reshape(rows, p))
    abar_re, abar_im, bbar_re, bbar_im = outs
    return (abar_re.reshape(g, n), abar_im.reshape(g, n),
            bbar_re.reshape(g, n, p), bbar_im.reshape(g, n, p))


def _state_layout(x_re, x_im):
    lead = x_re.shape[:-2]
    hg = SSM_GROUPS // 2
    r = x_re.reshape(lead + (2, 1, hg * SSM_STATE))
    i = x_im.reshape(lead + (2, 1, hg * SSM_STATE))
    return jnp.concatenate([r, i], axis=-2).reshape(lead + (STATE_LANES,))


def _state_unlayout(h):
    lead = h.shape[:-1]
    hg = SSM_GROUPS // 2
    x = h.reshape(lead + (2, 2, hg, SSM_STATE))
    re = x[..., :, 0, :, :].reshape(lead + (SSM_GROUPS, SSM_STATE))
    im = x[..., :, 1, :, :].reshape(lead + (SSM_GROUPS, SSM_STATE))
    return re, im


def _block_diag_b(bbar_re, bbar_im):
    hg = SSM_GROUPS // 2
    eye = jnp.eye(hg, dtype=F32)
    out = []
    for hf in range(2):
        parts = []
        for bb in (bbar_re, bbar_im):
            b = bb[hf * hg:(hf + 1) * hg]
            m = jnp.einsum('gnp,gk->gpkn', b, eye)
            parts.append(m.reshape(hg * SSM_GROUP, hg * SSM_STATE))
        out.append(jnp.concatenate(parts, axis=1))
    return jnp.stack(out).astype(BF16)


def _block_diag_c(c_re, c_im):
    hg = SSM_GROUPS // 2
    eye = jnp.eye(hg, dtype=F32)
    out = []
    for hf in range(2):
        parts = []
        for cc in (c_re, -c_im):
            c = cc[hf * hg:(hf + 1) * hg]
            m = jnp.einsum('gpn,gk->gnkp', c, eye)
            parts.append(m.reshape(hg * SSM_STATE, hg * SSM_GROUP))
        out.append(jnp.concatenate(parts, axis=0))
    return jnp.stack(out).astype(BF16)


def _inproj_kernel(x_ref, mod_ref, g_ref, wm_ref, wf_ref, bf_ref, gq_ref, gk_ref, e_ref,
                   u_ref, zs_ref, q_ref, kout_ref, vout_ref, katt_ref, vatt_ref, za_ref, logf_ref):
    x = x_ref[0]
    ms = jnp.mean(x * x, axis=-1, keepdims=True)
    xn = x * lax.rsqrt(ms + NORM_EPS) * g_ref[...]
    shift = mod_ref[0, 0:1, :]
    scale = mod_ref[0, 1:2, :]
    hb = (xn * (1.0 + scale) + shift).astype(BF16)

    def sec(i):
        return jnp.dot(hb, wm_ref[:, i * 512:(i + 1) * 512], preferred_element_type=F32)

    def head_rmsnorm(y, g):
        msq = jnp.dot((y * y).astype(BF16), e_ref[...], preferred_element_type=F32)
        return y * lax.rsqrt(msq + NORM_EPS) * g

    u_ref[0] = sec(0)
    zs_ref[0] = sec(1).astype(BF16)
    q = head_rmsnorm(sec(2), gq_ref[...])
    q_ref[0] = (q * ATTN_SCALE).astype(BF16)
    k = head_rmsnorm(sec(3), gk_ref[...])
    kout_ref[0] = k
    v = sec(4)
    vout_ref[0] = v
    kb = k.astype(BF16)
    vb = v.astype(BF16)
    for h in range(N_HEADS):
        katt_ref[0, h] = kb[:, h * HEAD_DIM:(h + 1) * HEAD_DIM]
        vatt_ref[0, h] = vb[:, h * HEAD_DIM:(h + 1) * HEAD_DIM]
    za_ref[0] = sec(5).astype(BF16)
    f = jnp.dot(hb, wf_ref[...], preferred_element_type=F32) + bf_ref[...]
    logf_ref[0] = jnp.minimum(f, 0.0) - jnp.log1p(jnp.exp(-jnp.abs(f)))


def _inproj(x, mod3, norm_g, w_main, w_f, b_f, gq, gk, e_mat, tm):
    b, t, d = x.shape
    nt = t // tm
    tok = lambda w, dt: jax.ShapeDtypeStruct((b, t, w), dt)
    heads = jax.ShapeDtypeStruct((b, N_HEADS, t, HEAD_DIM), BF16)
    const = lambda shape: pl.BlockSpec(shape, lambda i, j: (0,) * len(shape))
    tokspec = lambda w: pl.BlockSpec((1, tm, w), lambda i, j: (i, j, 0))
    headspec = pl.BlockSpec((1, N_HEADS, tm, HEAD_DIM), lambda i, j: (i, 0, j, 0))
    return pl.pallas_call(
        _inproj_kernel,
        grid=(b, nt),
        in_specs=[tokspec(d),
                  pl.BlockSpec((1, 3, d), lambda i, j: (i, 0, 0)),
                  const((1, d)), const(w_main.shape), const(w_f.shape), const((1, LANES)),
                  const((1, ATTN_WIDTH)), const((1, ATTN_WIDTH)), const(e_mat.shape)],
        out_specs=[tokspec(SSM_WIDTH), tokspec(SSM_WIDTH), tokspec(ATTN_WIDTH), tokspec(ATTN_WIDTH),
                   tokspec(ATTN_WIDTH), headspec, headspec, tokspec(ATTN_WIDTH), tokspec(LANES)],
        out_shape=[tok(SSM_WIDTH, F32), tok(SSM_WIDTH, BF16), tok(ATTN_WIDTH, BF16), tok(ATTN_WIDTH, F32),
                   tok(ATTN_WIDTH, F32), heads, heads, tok(ATTN_WIDTH, BF16), tok(LANES, F32)],
        compiler_params=_cparams(("arbitrary", "arbitrary")),
        name="inproj",
    )(x, mod3, norm_g, w_main, w_f, b_f, gq, gk, e_mat)


def _split3(x):
    hi = x.astype(BF16)
    r1 = x - hi.astype(F32)
    mid = r1.astype(BF16)
    lo = (r1 - mid.astype(F32)).astype(BF16)
    return hi, mid, lo


def _cumsum_kernel(x_ref, init_ref, tri_ref, o_ref, carry_ref):
    @pl.when(pl.program_id(1) == 0)
    def _():
        carry_ref[...] = init_ref[0]

    tri = tri_ref[...]
    c = carry_ref[...]
    for part in _split3(x_ref[0]):
        c = c + jnp.dot(tri, part, preferred_element_type=F32)
    o_ref[0] = c
    tc = o_ref.shape[1]
    carry_ref[...] = c[tc - 1:tc, :]


def _cumsum(x, init, tc):
    b, t, w = x.shape
    tri = jnp.tril(jnp.ones((tc, tc), F32)).astype(BF16)
    return pl.pallas_call(
        _cumsum_kernel,
        grid=(b, t // tc),
        in_specs=[pl.BlockSpec((1, tc, w), lambda i, j: (i, j, 0)),
                  pl.BlockSpec((1, 1, w), lambda i, j: (i, 0, 0)),
                  pl.BlockSpec((tc, tc), lambda i, j: (0, 0))],
        out_specs=pl.BlockSpec((1, tc, w), lambda i, j: (i, j, 0)),
        out_shape=jax.ShapeDtypeStruct((b, t, w), F32),
        scratch_shapes=[pltpu.VMEM((1, w), F32)],
        compiler_params=_cparams(("arbitrary", "arbitrary")),
        name="cumsum",
    )(x, init, tri)


def _s5_kernel(u_ref, h0_ref, are_ref, aim_ref, bblk_ref, cblk_ref, d_ref,
               y_ref, hout_ref,
               ut_ref, utm_ref, bu_ref, ytm_ref, *, tt, pitch):
    nb = u_ref.shape[0]
    nslab = SSM_WIDTH // LANES
    rows = tt * nb

    @pl.when(pl.program_id(0) == 0)
    def _():
        hout_ref[...] = h0_ref[...]

    for b in range(nb):
        for k in range(nslab):
            ut_ref[k, b * pitch:b * pitch + tt, :] = u_ref[b, :, k * LANES:(k + 1) * LANES]

    def gather(t, _):
        r = pl.multiple_of(t * nb, SUBLANES)
        for k in range(nslab):
            utm_ref[pl.ds(r, nb), k * LANES:(k + 1) * LANES] = ut_ref[k, pl.ds(t, nb, stride=pitch), :]
        return 0
    lax.fori_loop(0, tt, gather, 0)

    for hf in range(2):
        uh = utm_ref[:, hf * HALF_CH:(hf + 1) * HALF_CH].astype(BF16)
        bu_ref[:, hf * HALF_LANES:(hf + 1) * HALF_LANES] = jnp.dot(
            uh, bblk_ref[hf], preferred_element_type=F32)

    chunk = 4 * LANES
    quarter = HALF_LANES // 2
    for hf in range(2):
        for c in range(quarter // chunk):
            lo_re = hf * HALF_LANES + c * chunk
            lo_im = lo_re + quarter
            ar = are_ref[:, lo_re:lo_re + chunk]
            ai = aim_ref[:, lo_re:lo_re + chunk]

            def step(t, carry):
                xr, xi = carry
                r = pl.multiple_of(t * nb, SUBLANES)
                br = bu_ref[pl.ds(r, nb), lo_re:lo_re + chunk]
                bi = bu_ref[pl.ds(r, nb), lo_im:lo_im + chunk]
                nr = ar * xr - ai * xi + br
                ni = ar * xi + ai * xr + bi
                bu_ref[pl.ds(r, nb), lo_re:lo_re + chunk] = nr
                bu_ref[pl.ds(r, nb), lo_im:lo_im + chunk] = ni
                return nr, ni

            xr, xi = lax.fori_loop(
                0, tt, step,
                (hout_ref[:, lo_re:lo_re + chunk], hout_ref[:, lo_im:lo_im + chunk]), unroll=4)
            hout_ref[:, lo_re:lo_re + chunk] = xr
            hout_ref[:, lo_im:lo_im + chunk] = xi

    for hf in range(2):
        xh = bu_ref[:, hf * HALF_LANES:(hf + 1) * HALF_LANES].astype(BF16)
        yh = jnp.dot(xh, cblk_ref[hf], preferred_element_type=F32)
        yh = yh + d_ref[:, hf * HALF_CH:(hf + 1) * HALF_CH] * utm_ref[:, hf * HALF_CH:(hf + 1) * HALF_CH]
        for kk in range(HALF_CH // LANES):
            ytm_ref[hf * (HALF_CH // LANES) + kk] = yh[:, kk * LANES:(kk + 1) * LANES]
    for b in range(nb):
        for k in range(nslab):
            y_ref[b, :, k * LANES:(k + 1) * LANES] = ytm_ref[k, pl.ds(b, tt, stride=nb), :]
    del rows


def _s5(u, h0, are_b, aim_b, bblk, cblk, d_row, tt):
    b, t, w = u.shape
    assert b == SUBLANES and t % tt == 0 and tt % SUBLANES == 0
    pitch = tt + SUBLANES
    rows = tt * b
    const = lambda shape: pl.BlockSpec(shape, lambda j: (0,) * len(shape))
    return pl.pallas_call(
        functools.partial(_s5_kernel, tt=tt, pitch=pitch),
        grid=(t // tt,),
        in_specs=[pl.BlockSpec((b, tt, w), lambda j: (0, j, 0)),
                  const((b, STATE_LANES)), const((b, STATE_LANES)), const((b, STATE_LANES)),
                  const(bblk.shape), const(cblk.shape), const((1, w))],
        out_specs=[pl.BlockSpec((b, tt, w), lambda j: (0, j, 0)), const((b, STATE_LANES))],
        out_shape=[jax.ShapeDtypeStruct((b, t, w), F32), jax.ShapeDtypeStruct((b, STATE_LANES), F32)],
        scratch_shapes=[pltpu.VMEM((w // LANES, b * pitch, LANES), F32),
                        pltpu.VMEM((rows, w), F32),
                        pltpu.VMEM((rows, STATE_LANES), F32),
                        pltpu.VMEM((w // LANES, rows, LANES), F32)],
        compiler_params=_cparams(("arbitrary",)),
        name="s5",
    )(u, h0, are_b, aim_b, bblk, cblk, d_row)


def _attn_kernel(q_ref, k_ref, v_ref, cq_ref, ck_ref, za_ref, o_ref, *, blk, noff):
    hp = pl.program_id(1)
    qi = pl.program_id(2)
    nfull = noff + qi
    heads_per_step = q_ref.shape[2] // HEAD_DIM
    row = lax.broadcasted_iota(jnp.int32, (blk, blk), 0)
    col = lax.broadcasted_iota(jnp.int32, (blk, blk), 1)
    lane = lax.broadcasted_iota(jnp.int32, (blk, LANES), 1)
    cq_all = cq_ref[0]
    outs = []
    for hh in range(heads_per_step):
        q = q_ref[0, :, hh * HEAD_DIM:(hh + 1) * HEAD_DIM]
        h = hp * heads_per_step + hh
        cq = jnp.sum(jnp.where(lane == h, cq_all, 0.0), axis=-1, keepdims=True)

        def scores(j):
            start = pl.multiple_of(j * blk, blk)
            k = k_ref[0, hh, pl.ds(start, blk), :]
            s = lax.dot_general(q, k, (((1,), (1,)), ((), ())), preferred_element_type=F32)
            return s + cq - ck_ref[0, hh, j], start

        def update(s, start, carry):
            m, l, acc = carry
            v = v_ref[0, hh, pl.ds(start, blk), :]
            m_new = jnp.maximum(m, jnp.max(s, axis=-1, keepdims=True))
            alpha = jnp.exp(m - m_new)
            p = jnp.exp(s - m_new)
            l = alpha * l + jnp.sum(p, axis=-1, keepdims=True)
            acc = alpha * acc + jnp.dot(p.astype(BF16), v, preferred_element_type=F32)
            return m_new, l, acc

        def body(j, carry):
            s, start = scores(j)
            return update(s, start, carry)

        init = (jnp.full((blk, 1), NEG_BIG, F32), jnp.zeros((blk, 1), F32),
                jnp.zeros((blk, HEAD_DIM), F32))
        carry = lax.fori_loop(0, nfull, body, init)
        s, start = scores(nfull)
        s = jnp.where(row >= col, s, NEG_BIG)
        _, l, acc = update(s, start, carry)
        outs.append(acc / l)
    o = jnp.concatenate(outs, axis=-1)
    o_ref[0] = (o * _silu(za_ref[0].astype(F32))).astype(BF16)


def _attn(q, k_att, v_att, cq, ck, za, blk):
    b, tq, _ = q.shape
    tk = k_att.shape[2]
    hps = LANES // HEAD_DIM
    noff = (tk - tq) // blk
    nkb = tk // blk
    return pl.pallas_call(
        functools.partial(_attn_kernel, blk=blk, noff=noff),
        grid=(b, N_HEADS // hps, tq // blk),
        in_specs=[pl.BlockSpec((1, blk, LANES), lambda i, h, j: (i, j, h)),
                  pl.BlockSpec((1, hps, tk, HEAD_DIM), lambda i, h, j: (i, h, 0, 0)),
                  pl.BlockSpec((1, hps, tk, HEAD_DIM), lambda i, h, j: (i, h, 0, 0)),
                  pl.BlockSpec((1, blk, LANES), lambda i, h, j: (i, j, 0)),
                  pl.BlockSpec((1, hps, nkb, 1, blk), lambda i, h, j: (i, h, 0, 0, 0)),
                  pl.BlockSpec((1, blk, LANES), lambda i, h, j: (i, j, h))],
        out_specs=pl.BlockSpec((1, blk, LANES), lambda i, h, j: (i, j, h)),
        out_shape=jax.ShapeDtypeStruct((b, tq, ATTN_WIDTH), BF16),
        compiler_params=_cparams(("arbitrary", "arbitrary", "arbitrary")),
        name="attn",
    )(q, k_att, v_att, cq, ck, za)


def _out_kernel(x_ref, mod_ref, ys_ref, zs_ref, ya_ref, wglu_ref, bglu_ref, wo_ref, o_ref):
    y = ys_ref[0]
    g = y * (0.5 * (1.0 + jnp.tanh(GELU_C * (y + 0.044715 * (y * y * y)))))
    glu = jnp.dot(g.astype(BF16), wglu_ref[...], preferred_element_type=F32) + bglu_ref[...]
    y1 = g * _sigmoid(glu) * _silu(zs_ref[0].astype(F32))
    mixed = (jnp.dot(y1.astype(BF16), wo_ref[0:SSM_WIDTH, :], preferred_element_type=F32)
             + jnp.dot(ya_ref[0], wo_ref[SSM_WIDTH:, :], preferred_element_type=F32))
    o_ref[0] = x_ref[0] + mod_ref[0, 2:3, :] * mixed


def _out(x, mod3, y_ssm, zs, y_att, w_glu, b_glu, w_out, tm):
    b, t, d = x.shape
    const = lambda shape: pl.BlockSpec(shape, lambda i, j: (0,) * len(shape))
    tokspec = lambda w: pl.BlockSpec((1, tm, w), lambda i, j: (i, j, 0))
    return pl.pallas_call(
        _out_kernel,
        grid=(b, t // tm),
        in_specs=[tokspec(d), pl.BlockSpec((1, 3, d), lambda i, j: (i, 0, 0)),
                  tokspec(SSM_WIDTH), tokspec(SSM_WIDTH), tokspec(ATTN_WIDTH),
                  const(w_glu.shape), const((1, SSM_WIDTH)), const(w_out.shape)],
        out_specs=tokspec(d),
        out_shape=jax.ShapeDtypeStruct((b, t, d), F32),
        compiler_params=_cparams(("arbitrary", "arbitrary")),
        name="out",
    )(x, mod3, y_ssm, zs, y_att, w_glu, b_glu, w_out)


def _tile(t, pref):
    return pref if t % pref == 0 else t


def _layer(x, mod3, h0, past, p):
    b, t, d = x.shape
    tm = _tile(t, 512)
    u, zs, q, k_out, v_out, k_att, v_att, za, logf_pad = _inproj(
        x, mod3, p['norm_g'], p['w_main'], p['w_f'], p['b_f'], p['gq'], p['gk'], p['e_mat'], tm)

    y_ssm, h_new = _s5(u, h0, p['are_b'], p['aim_b'], p['bblk'], p['cblk'], p['d_row'], _tile(t, 64))

    zero_init = jnp.zeros((b, 1, LANES), F32)
    if past is None:
        blk = _tile(t, 512)
        cum = _cumsum(logf_pad, zero_init, blk)
        cum_k = cum
    else:
        ck_past, cv_past, clogf = past
        tp = ck_past.shape[1]
        blk = t
        clogf_pad = jnp.pad(clogf.astype(F32), ((0, 0), (0, 0), (0, LANES - N_HEADS)))
        cum_past = _cumsum(clogf_pad, zero_init, _tile(tp, 512))
        cum = _cumsum(logf_pad, cum_past[:, tp - 1:tp, :], blk)
        cum_k = jnp.concatenate([cum_past, cum], axis=1)
        k_att = jnp.concatenate([jnp.transpose(ck_past, (0, 2, 1, 3)).astype(BF16), k_att], axis=2)
        v_att = jnp.concatenate([jnp.transpose(cv_past, (0, 2, 1, 3)).astype(BF16), v_att], axis=2)
    tk = cum_k.shape[1]
    ck = jnp.transpose(cum_k[:, :, :N_HEADS], (0, 2, 1)).reshape(b, N_HEADS, tk // blk, 1, blk)
    y_att = _attn(q, k_att, v_att, cum, ck, za, blk)

    y = _out(x, mod3, y_ssm, zs, y_att, p['w_glu'], p['b_glu'], p['w_out'], tm)
    return y, k_out, v_out, logf_pad[:, :, :N_HEADS], h_new


def kernel(x_prompt, x_sample, cache_k, cache_v, cache_logf, state_ssm_re, state_ssm_im, c_prompt, c_sample,
           w_ada, b_ada, norm_g, w_in, b_f, q_norm_g, k_norm_g, ssm_log_dt, ssm_a_re, ssm_a_im,
           ssm_b_re, ssm_b_im, ssm_c_re, ssm_c_im, ssm_d, w_glu, b_glu, w_out):
    depth = w_ada.shape[0]
    d = x_prompt.shape[-1]
    bp = x_prompt.shape[0]
    bs = x_sample.shape[0]
    xp, xs = x_prompt, x_sample
    outs_p, outs_s = [], []
    for l in range(depth):
        mod = _mod(jnp.concatenate([c_prompt, c_sample], axis=0), w_ada[l], b_ada[l])
        mod3 = mod.reshape(bp + bs, 3, d)
        abar_re, abar_im, bbar_re, bbar_im = _zoh(ssm_log_dt[l], ssm_a_re[l], ssm_a_im[l],
                                                  ssm_b_re[l], ssm_b_im[l])
        a_lanes_re = _state_layout(abar_re, abar_re)
        a_lanes_im = _state_layout(abar_im, abar_im)
        split = 2 * SSM_WIDTH + 4 * ATTN_WIDTH
        hd = jnp.arange(ATTN_WIDTH) // HEAD_DIM
        p = dict(
            norm_g=norm_g[l].reshape(1, d),
            w_main=w_in[l][:, :split].astype(BF16),
            w_f=jnp.pad(w_in[l][:, split:], ((0, 0), (0, LANES - N_HEADS))).astype(BF16),
            b_f=jnp.pad(b_f[l], (0, LANES - N_HEADS)).reshape(1, LANES),
            gq=jnp.tile(q_norm_g[l], N_HEADS).reshape(1, ATTN_WIDTH),
            gk=jnp.tile(k_norm_g[l], N_HEADS).reshape(1, ATTN_WIDTH),
            e_mat=(hd[:, None] == hd[None, :]).astype(BF16) * (1.0 / HEAD_DIM),
            are_b=jnp.broadcast_to(a_lanes_re[None], (SUBLANES, STATE_LANES)),
            aim_b=jnp.broadcast_to(a_lanes_im[None], (SUBLANES, STATE_LANES)),
            bblk=_block_diag_b(bbar_re, bbar_im),
            cblk=_block_diag_c(ssm_c_re[l], ssm_c_im[l]),
            d_row=ssm_d[l].reshape(1, SSM_WIDTH),
            w_glu=w_glu[l].astype(BF16), b_glu=b_glu[l].reshape(1, SSM_WIDTH),
            w_out=w_out[l].astype(BF16),
        )
        h0_p = jnp.zeros((bp, STATE_LANES), F32)
        xp, k1, v1, f1, h1 = _layer(xp, mod3[:bp], h0_p, None, p)
        h0_s = _state_layout(state_ssm_re[l], state_ssm_im[l])
        xs, k2, v2, f2, h2 = _layer(xs, mod3[bp:], h0_s, (cache_k[l], cache_v[l], cache_logf[l]), p)
        outs_p.append((k1, v1, f1) + _state_unlayout(h1))
        outs_s.append((k2, v2, f2) + _state_unlayout(h2))

    def stack(items, i, shape_tail):
        arr = jnp.stack([it[i] for it in items])
        return arr.reshape(arr.shape[:3] + shape_tail) if shape_tail else arr

    hd_tail = (N_HEADS, HEAD_DIM)
    return (xp, xs,
            stack(outs_p, 0, hd_tail), stack(outs_p, 1, hd_tail), stack(outs_p, 2, ()),
            stack(outs_p, 3, ()), stack(outs_p, 4, ()),
            stack(outs_s, 0, hd_tail), stack(outs_s, 1, hd_tail), stack(outs_s, 2, ()),
            stack(outs_s, 3, ()), stack(outs_s, 4, ()))
```

```python
import functools

import numpy as np
import jax
import jax.numpy as jnp
from jax import lax
from jax.experimental import pallas as pl
from jax.experimental.pallas import tpu as pltpu

F32 = jnp.float32
BF16 = jnp.bfloat16

N_HEADS = 8
HEAD_DIM = 64
ATTN_WIDTH = N_HEADS * HEAD_DIM
SSM_GROUPS = 32
SSM_GROUP = 16
SSM_STATE = 64
SSM_WIDTH = SSM_GROUPS * SSM_GROUP
STATE_LANES = 2 * SSM_GROUPS * SSM_STATE
HALF_LANES = STATE_LANES // 2
HALF_CH = SSM_WIDTH // 2
ATTN_SCALE = HEAD_DIM ** -0.5
NORM_EPS = 1e-6
LANES = 128
SUBLANES = 8
NEG_BIG = -1e30
GELU_C = float(np.float32(np.sqrt(2.0 / np.pi)))
VMEM_LIMIT = 56 * 1024 * 1024
SKIP_LOGIT_GAP = 105.0
BF16_NORM_SLACK = 1.02


def _cparams(sem):
    return pltpu.CompilerParams(dimension_semantics=sem, vmem_limit_bytes=VMEM_LIMIT)


def _sigmoid(x):
    return 1.0 / (1.0 + jnp.exp(-x))


def _silu(x):
    return x * _sigmoid(x)


def _split3(x):
    hi = x.astype(BF16)
    r1 = x - hi.astype(F32)
    mid = r1.astype(BF16)
    lo = (r1 - mid.astype(F32)).astype(BF16)
    return hi, mid, lo


def _tri(n):
    return jnp.tril(jnp.ones((n, n), F32)).astype(BF16)


def _mod_kernel(c_ref, w_ref, b_ref, o_ref):
    c = c_ref[...]
    o_ref[...] = jnp.dot(_silu(c), w_ref[...], preferred_element_type=F32) + b_ref[...]


def _mod(c, w_ada, b_ada):
    n, d = c.shape
    n_out = w_ada.shape[1]
    blk = 1024
    return pl.pallas_call(
        _mod_kernel,
        grid=(n_out // blk,),
        in_specs=[pl.BlockSpec((n, d), lambda j: (0, 0)),
                  pl.BlockSpec((d, blk), lambda j: (0, j)),
                  pl.BlockSpec((1, blk), lambda j: (0, j))],
        out_specs=pl.BlockSpec((n, blk), lambda j: (0, j)),
        out_shape=jax.ShapeDtypeStruct((n, n_out), F32),
        compiler_params=_cparams(("arbitrary",)),
        name="mod",
    )(c, w_ada, b_ada.reshape(1, n_out))


def _zoh_kernel(ldt_ref, are_ref, aim_ref, bre_ref, bim_ref,
                abre_ref, abim_ref, bbre_ref, bbim_ref):
    dt = jnp.exp(ldt_ref[...])
    a_re = are_ref[...]
    a_im = aim_ref[...]
    mag = jnp.exp(a_re * dt)
    ang = a_im * dt
    abar_re = mag * jnp.cos(ang)
    abar_im = mag * jnp.sin(ang)
    den = a_re * a_re + a_im * a_im
    n_re = abar_re - 1.0
    n_im = abar_im
    q_re = (n_re * a_re + n_im * a_im) / den
    q_im = (n_im * a_re - n_re * a_im) / den
    b_re = bre_ref[...]
    b_im = bim_ref[...]
    abre_ref[...] = abar_re
    abim_ref[...] = abar_im
    bbre_ref[...] = q_re * b_re - q_im * b_im
    bbim_ref[...] = q_re * b_im + q_im * b_re


def _zoh(log_dt, a_re, a_im, b_re, b_im):
    g, n, p = b_re.shape
    rows = g * n
    ldt = jnp.broadcast_to(log_dt[:, None], (g, n)).reshape(rows, 1)
    outs = pl.pallas_call(
        _zoh_kernel,
        out_shape=[jax.ShapeDtypeStruct((rows, 1), F32), jax.ShapeDtypeStruct((rows, 1), F32),
                   jax.ShapeDtypeStruct((rows, p), F32), jax.ShapeDtypeStruct((rows, p), F32)],
        name="zoh",
    )(ldt, a_re.reshape(rows, 1), a_im.reshape(rows, 1), b_re.reshape(rows, p), b_im.reshape(rows, p))
    abar_re, abar_im, bbar_re, bbar_im = outs
    return (abar_re.reshape(g, n), abar_im.reshape(g, n),
            bbar_re.reshape(g, n, p), bbar_im.reshape(g, n, p))


def _state_layout(x_re, x_im):
    lead = x_re.shape[:-2]
    hg = SSM_GROUPS // 2
    r = x_re.reshape(lead + (2, 1, hg * SSM_STATE))
    i = x_im.reshape(lead + (2, 1, hg * SSM_STATE))
    return jnp.concatenate([r, i], axis=-2).reshape(lead + (STATE_LANES,))


def _state_unlayout(h):
    lead = h.shape[:-1]
    hg = SSM_GROUPS // 2
    x = h.reshape(lead + (2, 2, hg, SSM_STATE))
    re = x[..., :, 0, :, :].reshape(lead + (SSM_GROUPS, SSM_STATE))
    im = x[..., :, 1, :, :].reshape(lead + (SSM_GROUPS, SSM_STATE))
    return re, im


def _block_diag_b(bbar_re, bbar_im):
    hg = SSM_GROUPS // 2
    eye = jnp.eye(hg, dtype=F32)
    out = []
    for hf in range(2):
        parts = []
        for bb in (bbar_re, bbar_im):
            b = bb[hf * hg:(hf + 1) * hg]
            m = jnp.einsum('gnp,gk->gpkn', b, eye)
            parts.append(m.reshape(hg * SSM_GROUP, hg * SSM_STATE))
        out.append(jnp.concatenate(parts, axis=1))
    return jnp.stack(out).astype(BF16)


def _block_diag_c(c_re, c_im):
    hg = SSM_GROUPS // 2
    eye = jnp.eye(hg, dtype=F32)
    out = []
    for hf in range(2):
        parts = []
        for cc in (c_re, -c_im):
            c = cc[hf * hg:(hf + 1) * hg]
            m = jnp.einsum('gpn,gk->gnkp', c, eye)
            parts.append(m.reshape(hg * SSM_STATE, hg * SSM_GROUP))
        out.append(jnp.concatenate(parts, axis=0))
    return jnp.stack(out).astype(BF16)


def _inproj_kernel(x_ref, mod_ref, g_ref, wm_ref, wf_ref, bf_ref, gq_ref, gk_ref, e_ref, tri_ref, cinit_ref,
                   u_ref, zs_ref, q_ref, kout_ref, vout_ref, katt_ref, vatt_ref, za_ref, logf_ref, cum_ref,
                   carry_ref):
    @pl.when(pl.program_id(1) == 0)
    def _():
        carry_ref[...] = cinit_ref[0]

    x = x_ref[0]
    ms = jnp.mean(x * x, axis=-1, keepdims=True)
    xn = x * lax.rsqrt(ms + NORM_EPS) * g_ref[...]
    shift = mod_ref[0, 0:1, :]
    scale = mod_ref[0, 1:2, :]
    hb = (xn * (1.0 + scale) + shift).astype(BF16)

    def sec(i):
        return jnp.dot(hb, wm_ref[:, i * 512:(i + 1) * 512], preferred_element_type=F32)

    def head_rmsnorm(y, g):
        msq = jnp.dot((y * y).astype(BF16), e_ref[...], preferred_element_type=F32)
        return y * lax.rsqrt(msq + NORM_EPS) * g

    u_ref[0] = sec(0)
    zs_ref[0] = sec(1).astype(BF16)
    q = head_rmsnorm(sec(2), gq_ref[...])
    q_ref[0] = (q * ATTN_SCALE).astype(BF16)
    k = head_rmsnorm(sec(3), gk_ref[...])
    kout_ref[0] = k
    v = sec(4)
    vout_ref[0] = v
    kb = k.astype(BF16)
    vb = v.astype(BF16)
    for h in range(N_HEADS):
        katt_ref[0, h] = kb[:, h * HEAD_DIM:(h + 1) * HEAD_DIM]
        vatt_ref[0, h] = vb[:, h * HEAD_DIM:(h + 1) * HEAD_DIM]
    za_ref[0] = sec(5).astype(BF16)
    f = jnp.dot(hb, wf_ref[...], preferred_element_type=F32) + bf_ref[...]
    logf = jnp.minimum(f, 0.0) - jnp.log1p(jnp.exp(-jnp.abs(f)))
    logf_ref[0] = logf[:, :N_HEADS]
    c = carry_ref[...]
    for part in _split3(logf):
        c = c + jnp.dot(tri_ref[...], part, preferred_element_type=F32)
    cum_ref[0] = c[:, :N_HEADS]
    tm = c.shape[0]
    carry_ref[...] = c[tm - 1:tm, :]


def _inproj(x, mod3, norm_g, w_main, w_f, b_f, gq, gk, e_mat, cum_init, tm):
    b, t, d = x.shape
    nt = t // tm
    tok = lambda w, dt: jax.ShapeDtypeStruct((b, t, w), dt)
    heads = jax.ShapeDtypeStruct((b, N_HEADS, t, HEAD_DIM), BF16)
    const = lambda shape: pl.BlockSpec(shape, lambda i, j: (0,) * len(shape))
    tokspec = lambda w: pl.BlockSpec((1, tm, w), lambda i, j: (i, j, 0))
    headspec = pl.BlockSpec((1, N_HEADS, tm, HEAD_DIM), lambda i, j: (i, 0, j, 0))
    return pl.pallas_call(
        _inproj_kernel,
        grid=(b, nt),
        in_specs=[tokspec(d),
                  pl.BlockSpec((1, 3, d), lambda i, j: (i, 0, 0)),
                  const((1, d)), const(w_main.shape), const(w_f.shape), const((1, LANES)),
                  const((1, ATTN_WIDTH)), const((1, ATTN_WIDTH)), const(e_mat.shape), const((tm, tm)),
                  pl.BlockSpec((1, 1, LANES), lambda i, j: (i, 0, 0))],
        out_specs=[tokspec(SSM_WIDTH), tokspec(SSM_WIDTH), tokspec(ATTN_WIDTH), tokspec(ATTN_WIDTH),
                   tokspec(ATTN_WIDTH), headspec, headspec, tokspec(ATTN_WIDTH), tokspec(N_HEADS),
                   tokspec(N_HEADS)],
        out_shape=[tok(SSM_WIDTH, F32), tok(SSM_WIDTH, BF16), tok(ATTN_WIDTH, BF16), tok(ATTN_WIDTH, F32),
                   tok(ATTN_WIDTH, F32), heads, heads, tok(ATTN_WIDTH, BF16), tok(N_HEADS, F32),
                   tok(N_HEADS, F32)],
        scratch_shapes=[pltpu.VMEM((1, LANES), F32)],
        compiler_params=_cparams(("arbitrary", "arbitrary")),
        name="inproj",
    )(x, mod3, norm_g, w_main, w_f, b_f, gq, gk, e_mat, _tri(tm), cum_init)


def _cumsum_kernel(x_ref, tri_ref, o_ref, carry_ref):
    @pl.when(pl.program_id(1) == 0)
    def _():
        carry_ref[...] = jnp.zeros_like(carry_ref)

    c = carry_ref[...]
    for part in _split3(x_ref[0]):
        c = c + jnp.dot(tri_ref[...], part, preferred_element_type=F32)
    o_ref[0] = c
    tc = o_ref.shape[1]
    carry_ref[...] = c[tc - 1:tc, :]


def _cumsum(x, tc):
    b, t, w = x.shape
    return pl.pallas_call(
        _cumsum_kernel,
        grid=(b, t // tc),
        in_specs=[pl.BlockSpec((1, tc, w), lambda i, j: (i, j, 0)),
                  pl.BlockSpec((tc, tc), lambda i, j: (0, 0))],
        out_specs=pl.BlockSpec((1, tc, w), lambda i, j: (i, j, 0)),
        out_shape=jax.ShapeDtypeStruct((b, t, w), F32),
        scratch_shapes=[pltpu.VMEM((1, w), F32)],
        compiler_params=_cparams(("arbitrary", "arbitrary")),
        name="cumsum",
    )(x, _tri(tc))


def _s5_kernel(u_ref, h0_ref, are_ref, aim_ref, bblk_ref, cblk_ref, d_ref,
               y_ref, hout_ref,
               ut_ref, utm_ref, bu_ref, ytm_ref, *, tt, pitch):
    nb = u_ref.shape[0]
    nslab = SSM_WIDTH // LANES

    @pl.when(pl.program_id(0) == 0)
    def _():
        hout_ref[...] = h0_ref[...]

    for b in range(nb):
        for k in range(nslab):
            ut_ref[k, b * pitch:b * pitch + tt, :] = u_ref[b, :, k * LANES:(k + 1) * LANES]

    def gather(t, _):
        r = pl.multiple_of(t * nb, SUBLANES)
        for k in range(nslab):
            utm_ref[pl.ds(r, nb), k * LANES:(k + 1) * LANES] = ut_ref[k, pl.ds(t, nb, stride=pitch), :]
        return 0
    lax.fori_loop(0, tt, gather, 0)

    for hf in range(2):
        uh = utm_ref[:, hf * HALF_CH:(hf + 1) * HALF_CH].astype(BF16)
        bu_ref[:, hf * HALF_LANES:(hf + 1) * HALF_LANES] = jnp.dot(
            uh, bblk_ref[hf], preferred_element_type=F32)

    chunk = 4 * LANES
    quarter = HALF_LANES // 2
    for hf in range(2):
        for c in range(quarter // chunk):
            lo_re = hf * HALF_LANES + c * chunk
            lo_im = lo_re + quarter
            ar = are_ref[:, lo_re:lo_re + chunk]
            ai = aim_ref[:, lo_re:lo_re + chunk]

            def step(t, carry):
                xr, xi = carry
                r = pl.multiple_of(t * nb, SUBLANES)
                br = bu_ref[pl.ds(r, nb), lo_re:lo_re + chunk]
                bi = bu_ref[pl.ds(r, nb), lo_im:lo_im + chunk]
                nr = ar * xr - ai * xi + br
                ni = ar * xi + ai * xr + bi
                bu_ref[pl.ds(r, nb), lo_re:lo_re + chunk] = nr
                bu_ref[pl.ds(r, nb), lo_im:lo_im + chunk] = ni
                return nr, ni

            xr, xi = lax.fori_loop(
                0, tt, step,
                (hout_ref[:, lo_re:lo_re + chunk], hout_ref[:, lo_im:lo_im + chunk]), unroll=4)
            hout_ref[:, lo_re:lo_re + chunk] = xr
            hout_ref[:, lo_im:lo_im + chunk] = xi

    for hf in range(2):
        xh = bu_ref[:, hf * HALF_LANES:(hf + 1) * HALF_LANES].astype(BF16)
        yh = jnp.dot(xh, cblk_ref[hf], preferred_element_type=F32)
        yh = yh + d_ref[:, hf * HALF_CH:(hf + 1) * HALF_CH] * utm_ref[:, hf * HALF_CH:(hf + 1) * HALF_CH]
        for kk in range(HALF_CH // LANES):
            ytm_ref[hf * (HALF_CH // LANES) + kk] = yh[:, kk * LANES:(kk + 1) * LANES]
    for b in range(nb):
        for k in range(nslab):
            y_ref[b, :, k * LANES:(k + 1) * LANES] = ytm_ref[k, pl.ds(b, tt, stride=nb), :]


def _s5(u, h0, are_b, aim_b, bblk, cblk, d_row, tt):
    b, t, w = u.shape
    assert b == SUBLANES and t % tt == 0 and tt % SUBLANES == 0
    pitch = tt + SUBLANES
    rows = tt * b
    const = lambda shape: pl.BlockSpec(shape, lambda j: (0,) * len(shape))
    return pl.pallas_call(
        functools.partial(_s5_kernel, tt=tt, pitch=pitch),
        grid=(t // tt,),
        in_specs=[pl.BlockSpec((b, tt, w), lambda j: (0, j, 0)),
                  const((b, STATE_LANES)), const((b, STATE_LANES)), const((b, STATE_LANES)),
                  const(bblk.shape), const(cblk.shape), const((1, w))],
        out_specs=[pl.BlockSpec((b, tt, w), lambda j: (0, j, 0)), const((b, STATE_LANES))],
        out_shape=[jax.ShapeDtypeStruct((b, t, w), F32), jax.ShapeDtypeStruct((b, STATE_LANES), F32)],
        scratch_shapes=[pltpu.VMEM((w // LANES, b * pitch, LANES), F32),
                        pltpu.VMEM((rows, w), F32),
                        pltpu.VMEM((rows, STATE_LANES), F32),
                        pltpu.VMEM((w // LANES, rows, LANES), F32)],
        compiler_params=_cparams(("arbitrary",)),
        name="s5",
    )(u, h0, are_b, aim_b, bblk, cblk, d_row)


def _attn_kernel(*refs, bq, has_past):
    if has_past:
        (q_ref, kc_ref, vc_ref, cq_ref, ckc_ref, cec_ref, gq_ref, gk_ref, za_ref,
         kp_ref, vp_ref, ckp_ref, cep_ref, o_ref) = refs
    else:
        q_ref, kc_ref, vc_ref, cq_ref, ckc_ref, cec_ref, gq_ref, gk_ref, za_ref, o_ref = refs
    hp = pl.program_id(1)
    qi = pl.program_id(2)
    heads_per_step = q_ref.shape[2] // HEAD_DIM
    row = lax.broadcasted_iota(jnp.int32, (bq, bq), 0)
    col = lax.broadcasted_iota(jnp.int32, (bq, bq), 1)
    cq_all = cq_ref[0]
    head_lane = lax.broadcasted_iota(jnp.int32, cq_all.shape, 1)

    smax = (jnp.max(jnp.abs(gq_ref[...]), axis=-1, keepdims=True)
            * jnp.max(jnp.abs(gk_ref[...]), axis=-1, keepdims=True)
            * (HEAD_DIM * ATTN_SCALE * BF16_NORM_SLACK * BF16_NORM_SLACK))
    skip_below = -(SKIP_LOGIT_GAP + 2.0 * smax)

    outs = []
    for hh in range(heads_per_step):
        q = q_ref[0, :, hh * HEAD_DIM:(hh + 1) * HEAD_DIM]
        h = hp * heads_per_step + hh
        cq = jnp.sum(jnp.where(head_lane == h, cq_all, 0.0), axis=-1, keepdims=True)
        cq_first = cq[0:1, :]

        def n_skippable(ce_ref, n_visible):
            ce = ce_ref[0, hh]
            idx = lax.broadcasted_iota(jnp.int32, ce.shape, 1)
            dead = jnp.logical_and(cq_first - ce < skip_below, idx < n_visible)
            return jnp.sum(dead.astype(jnp.int32))

        def update(s, v, carry):
            m, l, acc = carry
            m_new = jnp.maximum(m, jnp.max(s, axis=-1, keepdims=True))
            alpha = jnp.exp(m - m_new)
            p = jnp.exp(s - m_new)
            l = alpha * l + jnp.sum(p, axis=-1, keepdims=True)
            acc = alpha * acc + jnp.dot(p.astype(BF16), v, preferred_element_type=F32)
            return m_new, l, acc

        def scores(k_ref, ck_ref, j, bk):
            start = pl.multiple_of(j * bk, bk)
            k = k_ref[0, hh, pl.ds(start, bk), :]
            s = lax.dot_general(q, k, (((1,), (1,)), ((), ())), preferred_element_type=F32)
            return s + cq - ck_ref[0, hh, j], start

        carry = (jnp.full((bq, 1), NEG_BIG, F32), jnp.zeros((bq, 1), F32), jnp.zeros((bq, HEAD_DIM), F32))

        if has_past:
            bkp = ckp_ref.shape[4]
            npb = ckp_ref.shape[2]

            def body_past(j, carry):
                s, start = scores(kp_ref, ckp_ref, j, bkp)
                return update(s, vp_ref[0, hh, pl.ds(start, bkp), :], carry)

            carry = lax.fori_loop(n_skippable(cep_ref, npb), npb, body_past, carry)

        def body(j, carry):
            s, start = scores(kc_ref, ckc_ref, j, bq)
            return update(s, vc_ref[0, hh, pl.ds(start, bq), :], carry)

        carry = lax.fori_loop(n_skippable(cec_ref, qi), qi, body, carry)
        s, start = scores(kc_ref, ckc_ref, qi, bq)
        s = jnp.where(row >= col, s, NEG_BIG)
        _, l, acc = update(s, vc_ref[0, hh, pl.ds(start, bq), :], carry)
        outs.append(acc / l)
    o = jnp.concatenate(outs, axis=-1)
    o_ref[0] = (o * _silu(za_ref[0].astype(F32))).astype(BF16)


def _row_blocks(cum, blk):
    b, t, h = cum.shape
    rows = jnp.transpose(cum, (0, 2, 1)).reshape(b, h, t // blk, 1, blk)
    return rows, rows[:, :, :, 0, blk - 1].reshape(b, h, 1, t // blk)


def _attn(q, k_cur, v_cur, cum, za, gq, gk, bq, past=None):
    b, tq, _ = q.shape
    hps = LANES // HEAD_DIM
    nq = tq // bq
    ckc, cec = _row_blocks(cum, bq)
    qspec = pl.BlockSpec((1, bq, LANES), lambda i, h, j: (i, j, h))
    kvspec = lambda t: pl.BlockSpec((1, hps, t, HEAD_DIM), lambda i, h, j: (i, h, 0, 0))
    rowspec = lambda n, w: pl.BlockSpec((1, hps, n, 1, w), lambda i, h, j: (i, h, 0, 0, 0))
    endspec = lambda n: pl.BlockSpec((1, hps, 1, n), lambda i, h, j: (i, h, 0, 0))
    gspec = pl.BlockSpec((1, ATTN_WIDTH), lambda i, h, j: (0, 0))
    in_specs = [qspec, kvspec(tq), kvspec(tq), pl.BlockSpec((1, bq, N_HEADS), lambda i, h, j: (i, j, 0)),
                rowspec(nq, bq), endspec(nq), gspec, gspec, qspec]
    args = [q, k_cur, v_cur, cum, ckc, cec, gq, gk, za]
    if past is not None:
        k_past, v_past, cum_past, bkp = past
        tp = k_past.shape[2]
        ckp, cep = _row_blocks(cum_past, bkp)
        in_specs += [kvspec(tp), kvspec(tp), rowspec(tp // bkp, bkp), endspec(tp // bkp)]
        args += [k_past, v_past, ckp, cep]
    return pl.pallas_call(
        functools.partial(_attn_kernel, bq=bq, has_past=past is not None),
        grid=(b, N_HEADS // hps, nq),
        in_specs=in_specs,
        out_specs=qspec,
        out_shape=jax.ShapeDtypeStruct((b, tq, ATTN_WIDTH), BF16),
        compiler_params=_cparams(("arbitrary", "arbitrary", "arbitrary")),
        name="attn",
    )(*args)


def _out_kernel(x_ref, mod_ref, ys_ref, zs_ref, ya_ref, wglu_ref, bglu_ref, wo_ref, o_ref):
    y = ys_ref[0]
    g = y * (0.5 * (1.0 + jnp.tanh(GELU_C * (y + 0.044715 * (y * y * y)))))
    glu = jnp.dot(g.astype(BF16), wglu_ref[...], preferred_element_type=F32) + bglu_ref[...]
    y1 = g * _sigmoid(glu) * _silu(zs_ref[0].astype(F32))
    mixed = (jnp.dot(y1.astype(BF16), wo_ref[0:SSM_WIDTH, :], preferred_element_type=F32)
             + jnp.dot(ya_ref[0], wo_ref[SSM_WIDTH:, :], preferred_element_type=F32))
    o_ref[0] = x_ref[0] + mod_ref[0, 2:3, :] * mixed


def _out(x, mod3, y_ssm, zs, y_att, w_glu, b_glu, w_out, tm):
    b, t, d = x.shape
    const = lambda shape: pl.BlockSpec(shape, lambda i, j: (0,) * len(shape))
    tokspec = lambda w: pl.BlockSpec((1, tm, w), lambda i, j: (i, j, 0))
    return pl.pallas_call(
        _out_kernel,
        grid=(b, t // tm),
        in_specs=[tokspec(d), pl.BlockSpec((1, 3, d), lambda i, j: (i, 0, 0)),
                  tokspec(SSM_WIDTH), tokspec(SSM_WIDTH), tokspec(ATTN_WIDTH),
                  const(w_glu.shape), const((1, SSM_WIDTH)), const(w_out.shape)],
        out_specs=tokspec(d),
        out_shape=jax.ShapeDtypeStruct((b, t, d), F32),
        compiler_params=_cparams(("arbitrary", "arbitrary")),
        name="out",
    )(x, mod3, y_ssm, zs, y_att, w_glu, b_glu, w_out)


def _tile(t, pref):
    return pref if t % pref == 0 else t


def _layer(x, mod3, h0, past, p):
    b, t, d = x.shape
    tm = _tile(t, 512)
    if past is None:
        cum_init = jnp.zeros((b, 1, LANES), F32)
        attn_past = None
    else:
        ck_past, cv_past, clogf = past
        tp = ck_past.shape[1]
        clogf_pad = jnp.pad(clogf.astype(F32), ((0, 0), (0, 0), (0, LANES - N_HEADS)))
        cum_past = _cumsum(clogf_pad, _tile(tp, 512))
        cum_init = cum_past[:, tp - 1:tp, :]
        attn_past = (jnp.transpose(ck_past, (0, 2, 1, 3)).astype(BF16),
                     jnp.transpose(cv_past, (0, 2, 1, 3)).astype(BF16),
                     cum_past[:, :, :N_HEADS], _tile(tp, 512))

    u, zs, q, k_out, v_out, k_att, v_att, za, logf, cum = _inproj(
        x, mod3, p['norm_g'], p['w_main'], p['w_f'], p['b_f'], p['gq'], p['gk'], p['e_mat'], cum_init, tm)
    y_ssm, h_new = _s5(u, h0, p['are_b'], p['aim_b'], p['bblk'], p['cblk'], p['d_row'], _tile(t, 64))
    y_att = _attn(q, k_att, v_att, cum, za, p['gq'], p['gk'], _tile(t, 512), attn_past)
    y = _out(x, mod3, y_ssm, zs, y_att, p['w_glu'], p['b_glu'], p['w_out'], tm)
    return y, k_out, v_out, logf, h_new


def kernel(x_prompt, x_sample, cache_k, cache_v, cache_logf, state_ssm_re, state_ssm_im, c_prompt, c_sample,
           w_ada, b_ada, norm_g, w_in, b_f, q_norm_g, k_norm_g, ssm_log_dt, ssm_a_re, ssm_a_im,
           ssm_b_re, ssm_b_im, ssm_c_re, ssm_c_im, ssm_d, w_glu, b_glu, w_out):
    depth = w_ada.shape[0]
    d = x_prompt.shape[-1]
    bp = x_prompt.shape[0]
    bs = x_sample.shape[0]
    xp, xs = x_prompt, x_sample
    outs_p, outs_s = [], []
    for l in range(depth):
        mod = _mod(jnp.concatenate([c_prompt, c_sample], axis=0), w_ada[l], b_ada[l])
        mod3 = mod.reshape(bp + bs, 3, d)
        abar_re, abar_im, bbar_re, bbar_im = _zoh(ssm_log_dt[l], ssm_a_re[l], ssm_a_im[l],
                                                  ssm_b_re[l], ssm_b_im[l])
        a_lanes_re = _state_layout(abar_re, abar_re)
        a_lanes_im = _state_layout(abar_im, abar_im)
        split = 2 * SSM_WIDTH + 4 * ATTN_WIDTH
        hd = jnp.arange(ATTN_WIDTH) // HEAD_DIM
        p = dict(
            norm_g=norm_g[l].reshape(1, d),
            w_main=w_in[l][:, :split].astype(BF16),
            w_f=jnp.pad(w_in[l][:, split:], ((0, 0), (0, LANES - N_HEADS))).astype(BF16),
            b_f=jnp.pad(b_f[l], (0, LANES - N_HEADS)).reshape(1, LANES),
            gq=jnp.tile(q_norm_g[l], N_HEADS).reshape(1, ATTN_WIDTH),
            gk=jnp.tile(k_norm_g[l], N_HEADS).reshape(1, ATTN_WIDTH),
            e_mat=(hd[:, None] == hd[None, :]).astype(BF16) * (1.0 / HEAD_DIM),
            are_b=jnp.broadcast_to(a_lanes_re[None], (SUBLANES, STATE_LANES)),
            aim_b=jnp.broadcast_to(a_lanes_im[None], (SUBLANES, STATE_LANES)),
            bblk=_block_diag_b(bbar_re, bbar_im),
            cblk=_block_diag_c(ssm_c_re[l], ssm_c_im[l]),
            d_row=ssm_d[l].reshape(1, SSM_WIDTH),
            w_glu=w_glu[l].astype(BF16), b_glu=b_glu[l].reshape(1, SSM_WIDTH),
            w_out=w_out[l].astype(BF16),
        )
        h0_p = jnp.zeros((bp, STATE_LANES), F32)
        xp, k1, v1, f1, h1 = _layer(xp, mod3[:bp], h0_p, None, p)
        h0_s = _state_layout(state_ssm_re[l], state_ssm_im[l])
        xs, k2, v2, f2, h2 = _layer(xs, mod3[bp:], h0_s, (cache_k[l], cache_v[l], cache_logf[l]), p)
        outs_p.append((k1, v1, f1) + _state_unlayout(h1))
        outs_s.append((k2, v2, f2) + _state_unlayout(h2))

    def stack(items, i, shape_tail):
        arr = jnp.stack([it[i] for it in items])
        return arr.reshape(arr.shape[:3] + shape_tail) if shape_tail else arr

    hd_tail = (N_HEADS, HEAD_DIM)
    return (xp, xs,
            stack(outs_p, 0, hd_tail), stack(outs_p, 1, hd_tail), stack(outs_p, 2, ()),
            stack(outs_p, 3, ()), stack(outs_p, 4, ()),
            stack(outs_s, 0, hd_tail), stack(outs_s, 1, hd_tail), stack(outs_s, 2, ()),
            stack(outs_s, 3, ()), stack(outs_s, 4, ()))
```

```python
import functools

import numpy as np
import jax
import jax.numpy as jnp
from jax import lax
from jax.experimental import pallas as pl
from jax.experimental.pallas import tpu as pltpu

F32 = jnp.float32
BF16 = jnp.bfloat16

N_HEADS = 8
HEAD_DIM = 64
ATTN_WIDTH = N_HEADS * HEAD_DIM
SSM_GROUPS = 32
SSM_GROUP = 16
SSM_STATE = 64
SSM_WIDTH = SSM_GROUPS * SSM_GROUP
STATE_LANES = 2 * SSM_GROUPS * SSM_STATE
HALF_LANES = STATE_LANES // 2
HALF_CH = SSM_WIDTH // 2
ATTN_SCALE = HEAD_DIM ** -0.5
NORM_EPS = 1e-6
LANES = 128
SUBLANES = 8
NEG_BIG = -1e30
LOG2E = float(np.log2(np.e))
GELU_C = float(np.float32(np.sqrt(2.0 / np.pi)))
VMEM_LIMIT = 56 * 1024 * 1024
SKIP_LOGIT_GAP = 105.0
BF16_NORM_SLACK = 1.02
S5_PHASES = 4
ATTN_SUB = 512
ATTN_DIAG_SUB = 512
BOUNDED_LOGIT_RANGE = 96.0


def _cparams(sem):
    return pltpu.CompilerParams(dimension_semantics=sem, vmem_limit_bytes=VMEM_LIMIT)


def _sigmoid(x):
    return 1.0 / (1.0 + jnp.exp(-x))


def _silu(x):
    return x * _sigmoid(x)


def _split3(x):
    hi = x.astype(BF16)
    r1 = x - hi.astype(F32)
    mid = r1.astype(BF16)
    lo = (r1 - mid.astype(F32)).astype(BF16)
    return hi, mid, lo


def _tri(n):
    return jnp.tril(jnp.ones((n, n), F32)).astype(BF16)


def _mod_kernel(c_ref, w_ref, b_ref, o_ref):
    c = c_ref[...]
    o_ref[...] = jnp.dot(_silu(c), w_ref[...], preferred_element_type=F32) + b_ref[...]


def _mod(c, w_ada, b_ada):
    n, d = c.shape
    n_out = w_ada.shape[1]
    blk = 1024
    return pl.pallas_call(
        _mod_kernel,
        grid=(n_out // blk,),
        in_specs=[pl.BlockSpec((n, d), lambda j: (0, 0)),
                  pl.BlockSpec((d, blk), lambda j: (0, j)),
                  pl.BlockSpec((1, blk), lambda j: (0, j))],
        out_specs=pl.BlockSpec((n, blk), lambda j: (0, j)),
        out_shape=jax.ShapeDtypeStruct((n, n_out), F32),
        compiler_params=_cparams(("arbitrary",)),
        name="mod",
    )(c, w_ada, b_ada.reshape(1, n_out))


def _zoh_kernel(ldt_ref, are_ref, aim_ref, bre_ref, bim_ref,
                abre_ref, abim_ref, bbre_ref, bbim_ref):
    dt = jnp.exp(ldt_ref[...])
    a_re = are_ref[...]
    a_im = aim_ref[...]
    mag = jnp.exp(a_re * dt)
    ang = a_im * dt
    abar_re = mag * jnp.cos(ang)
    abar_im = mag * jnp.sin(ang)
    den = a_re * a_re + a_im * a_im
    n_re = abar_re - 1.0
    n_im = abar_im
    q_re = (n_re * a_re + n_im * a_im) / den
    q_im = (n_im * a_re - n_re * a_im) / den
    b_re = bre_ref[...]
    b_im = bim_ref[...]
    abre_ref[...] = abar_re
    abim_ref[...] = abar_im
    bbre_ref[...] = q_re * b_re - q_im * b_im
    bbim_ref[...] = q_re * b_im + q_im * b_re


def _zoh(log_dt, a_re, a_im, b_re, b_im):
    g, n, p = b_re.shape
    rows = g * n
    ldt = jnp.broadcast_to(log_dt[:, None], (g, n)).reshape(rows, 1)
    outs = pl.pallas_call(
        _zoh_kernel,
        out_shape=[jax.ShapeDtypeStruct((rows, 1), F32), jax.ShapeDtypeStruct((rows, 1), F32),
                   jax.ShapeDtypeStruct((rows, p), F32), jax.ShapeDtypeStruct((rows, p), F32)],
        name="zoh",
    )(ldt, a_re.reshape(rows, 1), a_im.reshape(rows, 1), b_re.reshape(rows, p), b_im.reshape(rows, p))
    abar_re, abar_im, bbar_re, bbar_im = outs
    return (abar_re.reshape(g, n), abar_im.reshape(g, n),
            bbar_re.reshape(g, n, p), bbar_im.reshape(g, n, p))


def _state_layout(x_re, x_im):
    lead = x_re.shape[:-2]
    hg = SSM_GROUPS // 2
    r = x_re.reshape(lead + (2, 1, hg * SSM_STATE))
    i = x_im.reshape(lead + (2, 1, hg * SSM_STATE))
    return jnp.concatenate([r, i], axis=-2).reshape(lead + (STATE_LANES,))


def _state_unlayout(h):
    lead = h.shape[:-1]
    hg = SSM_GROUPS // 2
    x = h.reshape(lead + (2, 2, hg, SSM_STATE))
    re = x[..., :, 0, :, :].reshape(lead + (SSM_GROUPS, SSM_STATE))
    im = x[..., :, 1, :, :].reshape(lead + (SSM_GROUPS, SSM_STATE))
    return re, im


def _block_diag_b(bbar_re, bbar_im):
    hg = SSM_GROUPS // 2
    eye = jnp.eye(hg, dtype=F32)
    out = []
    for hf in range(2):
        parts = []
        for bb in (bbar_re, bbar_im):
            b = bb[hf * hg:(hf + 1) * hg]
            m = jnp.einsum('gnp,gk->gpkn', b, eye)
            parts.append(m.reshape(hg * SSM_GROUP, hg * SSM_STATE))
        out.append(jnp.concatenate(parts, axis=1))
    return jnp.stack(out).astype(BF16)


def _block_diag_c(c_re, c_im):
    hg = SSM_GROUPS // 2
    eye = jnp.eye(hg, dtype=F32)
    out = []
    for hf in range(2):
        parts = []
        for cc in (c_re, -c_im):
            c = cc[hf * hg:(hf + 1) * hg]
            m = jnp.einsum('gpn,gk->gnkp', c, eye)
            parts.append(m.reshape(hg * SSM_STATE, hg * SSM_GROUP))
        out.append(jnp.concatenate(parts, axis=0))
    return jnp.stack(out).astype(BF16)


def _inproj_kernel(x_ref, mod_ref, g_ref, wm_ref, wf_ref, bf_ref, gq_ref, gk_ref, e_ref, tri_ref, cinit_ref,
                   u_ref, zs_ref, q_ref, kout_ref, vout_ref, katt_ref, vatt_ref, za_ref, logf_ref, cum_ref,
                   carry_ref):
    @pl.when(pl.program_id(1) == 0)
    def _():
        carry_ref[...] = cinit_ref[0]

    x = x_ref[0]
    ms = jnp.mean(x * x, axis=-1, keepdims=True)
    xn = x * lax.rsqrt(ms + NORM_EPS) * g_ref[...]
    shift = mod_ref[0, 0:1, :]
    scale = mod_ref[0, 1:2, :]
    hb = (xn * (1.0 + scale) + shift).astype(BF16)

    def sec(i):
        return jnp.dot(hb, wm_ref[:, i * 512:(i + 1) * 512], preferred_element_type=F32)

    def head_rmsnorm(y, g):
        msq = jnp.dot((y * y).astype(BF16), e_ref[...], preferred_element_type=F32)
        return y * lax.rsqrt(msq + NORM_EPS) * g

    u_ref[0] = sec(0)
    zs_ref[0] = sec(1).astype(BF16)
    q = head_rmsnorm(sec(2), gq_ref[...])
    q_ref[0] = (q * (ATTN_SCALE * LOG2E)).astype(BF16)
    k = head_rmsnorm(sec(3), gk_ref[...])
    kout_ref[0] = k
    v = sec(4)
    vout_ref[0] = v
    kb = k.astype(BF16)
    vb = v.astype(BF16)
    for h in range(N_HEADS):
        katt_ref[0, h] = kb[:, h * HEAD_DIM:(h + 1) * HEAD_DIM]
        vatt_ref[0, h] = vb[:, h * HEAD_DIM:(h + 1) * HEAD_DIM]
    za_ref[0] = sec(5).astype(BF16)
    f = jnp.dot(hb, wf_ref[...], preferred_element_type=F32) + bf_ref[...]
    logf = jnp.minimum(f, 0.0) - jnp.log1p(jnp.exp(-jnp.abs(f)))
    logf_ref[0] = logf[:, :N_HEADS]
    c = carry_ref[...]
    for part in _split3(logf):
        c = c + jnp.dot(tri_ref[...], part, preferred_element_type=F32)
    cum_ref[0] = c[:, :N_HEADS]
    tm = c.shape[0]
    carry_ref[...] = c[tm - 1:tm, :]


def _inproj(x, mod3, norm_g, w_main, w_f, b_f, gq, gk, e_mat, cum_init, tm):
    b, t, d = x.shape
    nt = t // tm
    tok = lambda w, dt: jax.ShapeDtypeStruct((b, t, w), dt)
    heads = jax.ShapeDtypeStruct((b, N_HEADS, t, HEAD_DIM), BF16)
    const = lambda shape: pl.BlockSpec(shape, lambda i, j: (0,) * len(shape))
    tokspec = lambda w: pl.BlockSpec((1, tm, w), lambda i, j: (i, j, 0))
    headspec = pl.BlockSpec((1, N_HEADS, tm, HEAD_DIM), lambda i, j: (i, 0, j, 0))
    return pl.pallas_call(
        _inproj_kernel,
        grid=(b, nt),
        in_specs=[tokspec(d),
                  pl.BlockSpec((1, 3, d), lambda i, j: (i, 0, 0)),
                  const((1, d)), const(w_main.shape), const(w_f.shape), const((1, LANES)),
                  const((1, ATTN_WIDTH)), const((1, ATTN_WIDTH)), const(e_mat.shape), const((tm, tm)),
                  pl.BlockSpec((1, 1, LANES), lambda i, j: (i, 0, 0))],
        out_specs=[tokspec(SSM_WIDTH), tokspec(SSM_WIDTH), tokspec(ATTN_WIDTH), tokspec(ATTN_WIDTH),
                   tokspec(ATTN_WIDTH), headspec, headspec, tokspec(ATTN_WIDTH), tokspec(N_HEADS),
                   tokspec(N_HEADS)],
        out_shape=[tok(SSM_WIDTH, F32), tok(SSM_WIDTH, BF16), tok(ATTN_WIDTH, BF16), tok(ATTN_WIDTH, F32),
                   tok(ATTN_WIDTH, F32), heads, heads, tok(ATTN_WIDTH, BF16), tok(N_HEADS, F32),
                   tok(N_HEADS, F32)],
        scratch_shapes=[pltpu.VMEM((1, LANES), F32)],
        compiler_params=_cparams(("arbitrary", "arbitrary")),
        name="inproj",
    )(x, mod3, norm_g, w_main, w_f, b_f, gq, gk, e_mat, _tri(tm), cum_init)


def _cumsum_kernel(x_ref, tri_ref, o_ref, carry_ref):
    @pl.when(pl.program_id(1) == 0)
    def _():
        carry_ref[...] = jnp.zeros_like(carry_ref)

    c = carry_ref[...]
    for part in _split3(x_ref[0]):
        c = c + jnp.dot(tri_ref[...], part, preferred_element_type=F32)
    o_ref[0] = c
    tc = o_ref.shape[1]
    carry_ref[...] = c[tc - 1:tc, :]


def _cumsum(x, tc):
    b, t, w = x.shape
    return pl.pallas_call(
        _cumsum_kernel,
        grid=(b, t // tc),
        in_specs=[pl.BlockSpec((1, tc, w), lambda i, j: (i, j, 0)),
                  pl.BlockSpec((tc, tc), lambda i, j: (0, 0))],
        out_specs=pl.BlockSpec((1, tc, w), lambda i, j: (i, j, 0)),
        out_shape=jax.ShapeDtypeStruct((b, t, w), F32),
        scratch_shapes=[pltpu.VMEM((1, w), F32)],
        compiler_params=_cparams(("arbitrary", "arbitrary")),
        name="cumsum",
    )(x, _tri(tc))


def _s5_kernel(u_ref, h0_ref, are_ref, aim_ref, bblk_ref, cblk_ref, d_ref,
               y_ref, hout_ref,
               ut_ref, utm_ref, bu_ref, ytm_ref, *, tt, pitch):
    nb = u_ref.shape[0]
    nslab = SSM_WIDTH // LANES

    @pl.when(pl.program_id(0) == 0)
    def _():
        hout_ref[...] = h0_ref[...]

    for b in range(nb):
        for k in range(nslab):
            ut_ref[k, b * pitch:b * pitch + tt, :] = u_ref[b, :, k * LANES:(k + 1) * LANES]

    nph = S5_PHASES if tt % (S5_PHASES * SUBLANES) == 0 else 1
    tp = tt // nph
    chunk = 4 * LANES
    quarter = HALF_LANES // 2
    chunks = [(hf * HALF_LANES + c * chunk, hf * HALF_LANES + c * chunk + quarter)
              for hf in range(2) for c in range(quarter // chunk)]

    def b_proj(ph):
        for t in range(ph * tp, (ph + 1) * tp):
            for k in range(nslab):
                utm_ref[t * nb:(t + 1) * nb, k * LANES:(k + 1) * LANES] = ut_ref[k, pl.ds(t, nb, stride=pitch), :]
        rows = slice(ph * tp * nb, (ph + 1) * tp * nb)
        for hf in range(2):
            uh = utm_ref[rows, hf * HALF_CH:(hf + 1) * HALF_CH].astype(BF16)
            bu_ref[rows, hf * HALF_LANES:(hf + 1) * HALF_LANES] = jnp.dot(
                uh, bblk_ref[hf], preferred_element_type=F32)

    def scan(ph):
        state = [(hout_ref[:, lo_re:lo_re + chunk], hout_ref[:, lo_im:lo_im + chunk]) for lo_re, lo_im in chunks]
        for t in range(ph * tp, (ph + 1) * tp):
            r = slice(t * nb, (t + 1) * nb)
            for ci, (lo_re, lo_im) in enumerate(chunks):
                xr, xi = state[ci]
                ar = are_ref[:, lo_re:lo_re + chunk]
                ai = aim_ref[:, lo_re:lo_re + chunk]
                nr = ar * xr - ai * xi + bu_ref[r, lo_re:lo_re + chunk]
                ni = ar * xi + ai * xr + bu_ref[r, lo_im:lo_im + chunk]
                bu_ref[r, lo_re:lo_re + chunk] = nr
                bu_ref[r, lo_im:lo_im + chunk] = ni
                state[ci] = (nr, ni)
        for (lo_re, lo_im), (xr, xi) in zip(chunks, state):
            hout_ref[:, lo_re:lo_re + chunk] = xr
            hout_ref[:, lo_im:lo_im + chunk] = xi

    def c_proj(ph):
        rows = slice(ph * tp * nb, (ph + 1) * tp * nb)
        for hf in range(2):
            xh = bu_ref[rows, hf * HALF_LANES:(hf + 1) * HALF_LANES].astype(BF16)
            yh = jnp.dot(xh, cblk_ref[hf], preferred_element_type=F32)
            yh = yh + d_ref[:, hf * HALF_CH:(hf + 1) * HALF_CH] * utm_ref[rows, hf * HALF_CH:(hf + 1) * HALF_CH]
            for kk in range(HALF_CH // LANES):
                ytm_ref[hf * (HALF_CH // LANES) + kk, rows, :] = yh[:, kk * LANES:(kk + 1) * LANES]
        for b in range(nb):
            for k in range(nslab):
                y_ref[b, ph * tp:(ph + 1) * tp, k * LANES:(k + 1) * LANES] = (
                    ytm_ref[k, pl.ds(ph * tp * nb + b, tp, stride=nb), :])

    b_proj(0)
    for ph in range(nph):
        if ph + 1 < nph:
            b_proj(ph + 1)
        scan(ph)
        c_proj(ph)


def _s5(u, h0, are_b, aim_b, bblk, cblk, d_row, tt):
    b, t, w = u.shape
    assert b == SUBLANES and t % tt == 0 and tt % SUBLANES == 0
    pitch = tt + SUBLANES
    rows = tt * b
    const = lambda shape: pl.BlockSpec(shape, lambda j: (0,) * len(shape))
    return pl.pallas_call(
        functools.partial(_s5_kernel, tt=tt, pitch=pitch),
        grid=(t // tt,),
        in_specs=[pl.BlockSpec((b, tt, w), lambda j: (0, j, 0)),
                  const((b, STATE_LANES)), const((b, STATE_LANES)), const((b, STATE_LANES)),
                  const(bblk.shape), const(cblk.shape), const((1, w))],
        out_specs=[pl.BlockSpec((b, tt, w), lambda j: (0, j, 0)), const((b, STATE_LANES))],
        out_shape=[jax.ShapeDtypeStruct((b, t, w), F32), jax.ShapeDtypeStruct((b, STATE_LANES), F32)],
        scratch_shapes=[pltpu.VMEM((w // LANES, b * pitch, LANES), F32),
                        pltpu.VMEM((rows, w), F32),
                        pltpu.VMEM((rows, STATE_LANES), F32),
                        pltpu.VMEM((w // LANES, rows, LANES), F32)],
        compiler_params=_cparams(("arbitrary",)),
        name="s5",
    )(u, h0, are_b, aim_b, bblk, cblk, d_row)


def _attn_kernel(*refs, bq, has_past):
    if has_past:
        (q_ref, kc_ref, vc_ref, cq_ref, ckc_ref, cec_ref, gq_ref, gk_ref, za_ref,
         kp_ref, vp_ref, ckp_ref, cep_ref, o_ref) = refs
    else:
        q_ref, kc_ref, vc_ref, cq_ref, ckc_ref, cec_ref, gq_ref, gk_ref, za_ref, o_ref = refs
    hp = pl.program_id(1)
    qi = pl.program_id(2)
    heads_per_step = q_ref.shape[2] // HEAD_DIM
    heads = range(heads_per_step)
    cq_all = cq_ref[0]
    head_lane = lax.broadcasted_iota(jnp.int32, cq_all.shape, 1)

    smax = (jnp.max(jnp.abs(gq_ref[...]), axis=-1, keepdims=True)
            * jnp.max(jnp.abs(gk_ref[...]), axis=-1, keepdims=True)
            * (HEAD_DIM * ATTN_SCALE * BF16_NORM_SLACK * BF16_NORM_SLACK))
    skip_below = -(SKIP_LOGIT_GAP + 2.0 * smax)
    smax2 = smax * LOG2E

    qs = [q_ref[0, :, hh * HEAD_DIM:(hh + 1) * HEAD_DIM] for hh in heads]

    def cq_of(hh, rows):
        h = hp * heads_per_step + hh
        return jnp.sum(jnp.where(head_lane[rows] == h, cq_all[rows], 0.0), axis=-1, keepdims=True)

    def first_live(ce_ref, n_visible):
        dead = None
        for hh in heads:
            cq_first = cq_of(hh, slice(0, 1))
            ce = ce_ref[0, hh]
            idx = lax.broadcasted_iota(jnp.int32, ce.shape, 1)
            d = jnp.logical_and(cq_first - ce < skip_below, idx < n_visible)
            dead = d if dead is None else jnp.logical_and(dead, d)
        return jnp.sum(dead.astype(jnp.int32))

    def finish(outs):
        o = jnp.concatenate(outs, axis=-1)
        o_ref[0] = (o * _silu(za_ref[0].astype(F32))).astype(BF16)

    def bounded():
        sub = min(bq, ATTN_SUB)
        refs_row = [smax2 - cq_of(hh, slice(None)) * LOG2E for hh in heads]

        def piece(hh, rows, k_ref, v_ref, ck_ref, j, off, sub, diag_row0, carry_h):
            lp, acc = carry_h
            bk = ck_ref.shape[4]
            start = pl.multiple_of(j * bk + off, sub)
            k = k_ref[0, hh, pl.ds(start, sub), :]
            s = lax.dot_general(qs[hh][rows], k, (((1,), (1,)), ((), ())), preferred_element_type=F32)
            t = s - ck_ref[0, hh, j][:, off:off + sub] * LOG2E - refs_row[hh][rows]
            if diag_row0 is not None:
                r = lax.broadcasted_iota(jnp.int32, t.shape, 0) + diag_row0
                c = lax.broadcasted_iota(jnp.int32, t.shape, 1) + off
                t = jnp.where(r >= c, t, NEG_BIG)
            p = jnp.exp2(t)
            psum = p[:, 0:LANES]
            for c0 in range(LANES, sub, LANES):
                psum = psum + p[:, c0:c0 + LANES]
            pv = jnp.dot(p.astype(BF16), v_ref[0, hh, pl.ds(start, sub), :], preferred_element_type=F32)
            return lp + psum, acc + pv

        def block(k_ref, v_ref, ck_ref, j, carry):
            bk = ck_ref.shape[4]
            new = []
            for hh in heads:
                c = carry[hh]
                for off in range(0, bk, sub):
                    c = piece(hh, slice(None), k_ref, v_ref, ck_ref, j, off, sub, None, c)
                new.append(c)
            return tuple(new)

        def sweep(k_ref, v_ref, ck_ref, lo, hi, carry):
            odd = lo + jnp.bitwise_and(hi - lo, 1)
            carry = lax.fori_loop(lo, odd, lambda j, c: block(k_ref, v_ref, ck_ref, j, c), carry)

            def pair(i, c):
                j = odd + 2 * i
                return block(k_ref, v_ref, ck_ref, j + 1, block(k_ref, v_ref, ck_ref, j, c))
            return lax.fori_loop(0, lax.shift_right_logical(hi - odd, 1), pair, carry)

        lw = min(sub, LANES)
        carry = tuple((jnp.zeros((bq, lw), F32), jnp.zeros((bq, HEAD_DIM), F32)) for _ in heads)
        if has_past:
            npb = ckp_ref.shape[2]
            carry = sweep(kp_ref, vp_ref, ckp_ref, first_live(cep_ref, npb), npb, carry)
        carry = sweep(kc_ref, vc_ref, ckc_ref, first_live(cec_ref, qi), qi, carry)
        outs = []
        for hh in heads:
            lp, acc = carry[hh]
            dsub = min(bq, ATTN_DIAG_SUB)
            for off in range(0, bq, dsub):
                rows = slice(off, bq)
                lpr, accr = piece(hh, rows, kc_ref, vc_ref, ckc_ref, qi, off, dsub, off, (lp[rows], acc[rows]))
                if off:
                    lp = jnp.concatenate([lp[:off], lpr], axis=0)
                    acc = jnp.concatenate([acc[:off], accr], axis=0)
                else:
                    lp, acc = lpr, accr
            outs.append(acc / jnp.sum(lp, axis=-1, keepdims=True))
        finish(outs)

    def online():
        row = lax.broadcasted_iota(jnp.int32, (bq, bq), 0)
        col = lax.broadcasted_iota(jnp.int32, (bq, bq), 1)

        def update(s, v, carry_h):
            m, l, acc = carry_h
            m_new = jnp.maximum(m, jnp.max(s, axis=-1, keepdims=True))
            alpha = jnp.exp2(m - m_new)
            p = jnp.exp2(s - m_new)
            l = alpha * l + jnp.sum(p, axis=-1, keepdims=True)
            acc = alpha * acc + jnp.dot(p.astype(BF16), v, preferred_element_type=F32)
            return m_new, l, acc

        def scores(hh, k_ref, ck_ref, j):
            bk = ck_ref.shape[4]
            start = pl.multiple_of(j * bk, bk)
            k = k_ref[0, hh, pl.ds(start, bk), :]
            s = lax.dot_general(qs[hh], k, (((1,), (1,)), ((), ())), preferred_element_type=F32)
            return s - ck_ref[0, hh, j] * LOG2E, start, bk

        def block(k_ref, v_ref, ck_ref, j, carry):
            new = []
            for hh in heads:
                s, start, bk = scores(hh, k_ref, ck_ref, j)
                new.append(update(s, v_ref[0, hh, pl.ds(start, bk), :], carry[hh]))
            return tuple(new)

        carry = tuple((jnp.full((bq, 1), NEG_BIG, F32), jnp.zeros((bq, 1), F32), jnp.zeros((bq, HEAD_DIM), F32))
                      for _ in heads)
        if has_past:
            npb = ckp_ref.shape[2]
            carry = lax.fori_loop(first_live(cep_ref, npb), npb,
                                  lambda j, c: block(kp_ref, vp_ref, ckp_ref, j, c), carry)
        carry = lax.fori_loop(first_live(cec_ref, qi), qi,
                              lambda j, c: block(kc_ref, vc_ref, ckc_ref, j, c), carry)
        outs = []
        for hh in heads:
            s, start, bk = scores(hh, kc_ref, ckc_ref, qi)
            s = jnp.where(row >= col, s, NEG_BIG)
            _, l, acc = update(s, vc_ref[0, hh, pl.ds(start, bk), :], carry[hh])
            outs.append(acc / l)
        finish(outs)

    small_logits = jnp.max(smax2) * 2.0 <= BOUNDED_LOGIT_RANGE
    pl.when(small_logits)(bounded)
    pl.when(jnp.logical_not(small_logits))(online)


def _row_blocks(cum, blk):
    b, t, h = cum.shape
    rows = jnp.transpose(cum, (0, 2, 1)).reshape(b, h, t // blk, 1, blk)
    return rows, rows[:, :, :, 0, blk - 1].reshape(b, h, 1, t // blk)


def _attn(q, k_cur, v_cur, cum, za, gq, gk, bq, past=None):
    b, tq, _ = q.shape
    hps = LANES // HEAD_DIM
    nq = tq // bq
    ckc, cec = _row_blocks(cum, bq)
    qspec = pl.BlockSpec((1, bq, LANES), lambda i, h, j: (i, j, h))
    kvspec = lambda t: pl.BlockSpec((1, hps, t, HEAD_DIM), lambda i, h, j: (i, h, 0, 0))
    rowspec = lambda n, w: pl.BlockSpec((1, hps, n, 1, w), lambda i, h, j: (i, h, 0, 0, 0))
    endspec = lambda n: pl.BlockSpec((1, hps, 1, n), lambda i, h, j: (i, h, 0, 0))
    gspec = pl.BlockSpec((1, ATTN_WIDTH), lambda i, h, j: (0, 0))
    in_specs = [qspec, kvspec(tq), kvspec(tq), pl.BlockSpec((1, bq, N_HEADS), lambda i, h, j: (i, j, 0)),
                rowspec(nq, bq), endspec(nq), gspec, gspec, qspec]
    args = [q, k_cur, v_cur, cum, ckc, cec, gq, gk, za]
    if past is not None:
        k_past, v_past, cum_past, bkp = past
        tp = k_past.shape[2]
        ckp, cep = _row_blocks(cum_past, bkp)
        in_specs += [kvspec(tp), kvspec(tp), rowspec(tp // bkp, bkp), endspec(tp // bkp)]
        args += [k_past, v_past, ckp, cep]
    return pl.pallas_call(
        functools.partial(_attn_kernel, bq=bq, has_past=past is not None),
        grid=(b, N_HEADS // hps, nq),
        in_specs=in_specs,
        out_specs=qspec,
        out_shape=jax.ShapeDtypeStruct((b, tq, ATTN_WIDTH), BF16),
        compiler_params=_cparams(("arbitrary", "arbitrary", "arbitrary")),
        name="attn",
    )(*args)


def _out_kernel(x_ref, mod_ref, ys_ref, zs_ref, ya_ref, wglu_ref, bglu_ref, wo_ref, o_ref):
    y = ys_ref[0]
    g = y * (0.5 * (1.0 + jnp.tanh(GELU_C * (y + 0.044715 * (y * y * y)))))
    glu = jnp.dot(g.astype(BF16), wglu_ref[...], preferred_element_type=F32) + bglu_ref[...]
    y1 = g * _sigmoid(glu) * _silu(zs_ref[0].astype(F32))
    mixed = (jnp.dot(y1.astype(BF16), wo_ref[0:SSM_WIDTH, :], preferred_element_type=F32)
             + jnp.dot(ya_ref[0], wo_ref[SSM_WIDTH:, :], preferred_element_type=F32))
    o_ref[0] = x_ref[0] + mod_ref[0, 2:3, :] * mixed


def _out(x, mod3, y_ssm, zs, y_att, w_glu, b_glu, w_out, tm):
    b, t, d = x.shape
    const = lambda shape: pl.BlockSpec(shape, lambda i, j: (0,) * len(shape))
    tokspec = lambda w: pl.BlockSpec((1, tm, w), lambda i, j: (i, j, 0))
    return pl.pallas_call(
        _out_kernel,
        grid=(b, t // tm),
        in_specs=[tokspec(d), pl.BlockSpec((1, 3, d), lambda i, j: (i, 0, 0)),
                  tokspec(SSM_WIDTH), tokspec(SSM_WIDTH), tokspec(ATTN_WIDTH),
                  const(w_glu.shape), const((1, SSM_WIDTH)), const(w_out.shape)],
        out_specs=tokspec(d),
        out_shape=jax.ShapeDtypeStruct((b, t, d), F32),
        compiler_params=_cparams(("arbitrary", "arbitrary")),
        name="out",
    )(x, mod3, y_ssm, zs, y_att, w_glu, b_glu, w_out)


def _tile(t, pref):
    return pref if t % pref == 0 else t


def _layer(x, mod3, h0, past, p):
    b, t, d = x.shape
    tm = _tile(t, 512)
    if past is None:
        cum_init = jnp.zeros((b, 1, LANES), F32)
        attn_past = None
    else:
        ck_past, cv_past, clogf = past
        tp = ck_past.shape[1]
        clogf_pad = jnp.pad(clogf.astype(F32), ((0, 0), (0, 0), (0, LANES - N_HEADS)))
        cum_past = _cumsum(clogf_pad, _tile(tp, 512))
        cum_init = cum_past[:, tp - 1:tp, :]
        attn_past = (jnp.transpose(ck_past, (0, 2, 1, 3)).astype(BF16),
                     jnp.transpose(cv_past, (0, 2, 1, 3)).astype(BF16),
                     cum_past[:, :, :N_HEADS], _tile(tp, 512))

    u, zs, q, k_out, v_out, k_att, v_att, za, logf, cum = _inproj(
        x, mod3, p['norm_g'], p['w_main'], p['w_f'], p['b_f'], p['gq'], p['gk'], p['e_mat'], cum_init, tm)
    y_ssm, h_new = _s5(u, h0, p['are_b'], p['aim_b'], p['bblk'], p['cblk'], p['d_row'], _tile(t, 128))
    y_att = _attn(q, k_att, v_att, cum, za, p['gq'], p['gk'], _tile(t, 512), attn_past)
    y = _out(x, mod3, y_ssm, zs, y_att, p['w_glu'], p['b_glu'], p['w_out'], tm)
    return y, k_out, v_out, logf, h_new


def kernel(x_prompt, x_sample, cache_k, cache_v, cache_logf, state_ssm_re, state_ssm_im, c_prompt, c_sample,
           w_ada, b_ada, norm_g, w_in, b_f, q_norm_g, k_norm_g, ssm_log_dt, ssm_a_re, ssm_a_im,
           ssm_b_re, ssm_b_im, ssm_c_re, ssm_c_im, ssm_d, w_glu, b_glu, w_out):
    depth = w_ada.shape[0]
    d = x_prompt.shape[-1]
    bp = x_prompt.shape[0]
    bs = x_sample.shape[0]
    xp, xs = x_prompt, x_sample
    outs_p, outs_s = [], []
    for l in range(depth):
        mod = _mod(jnp.concatenate([c_prompt, c_sample], axis=0), w_ada[l], b_ada[l])
        mod3 = mod.reshape(bp + bs, 3, d)
        abar_re, abar_im, bbar_re, bbar_im = _zoh(ssm_log_dt[l], ssm_a_re[l], ssm_a_im[l],
                                                  ssm_b_re[l], ssm_b_im[l])
        a_lanes_re = _state_layout(abar_re, abar_re)
        a_lanes_im = _state_layout(abar_im, abar_im)
        split = 2 * SSM_WIDTH + 4 * ATTN_WIDTH
        hd = jnp.arange(ATTN_WIDTH) // HEAD_DIM
        p = dict(
            norm_g=norm_g[l].reshape(1, d),
            w_main=w_in[l][:, :split].astype(BF16),
            w_f=jnp.pad(w_in[l][:, split:], ((0, 0), (0, LANES - N_HEADS))).astype(BF16),
            b_f=jnp.pad(b_f[l], (0, LANES - N_HEADS)).reshape(1, LANES),
            gq=jnp.tile(q_norm_g[l], N_HEADS).reshape(1, ATTN_WIDTH),
            gk=jnp.tile(k_norm_g[l], N_HEADS).reshape(1, ATTN_WIDTH),
            e_mat=(hd[:, None] == hd[None, :]).astype(BF16) * (1.0 / HEAD_DIM),
            are_b=jnp.broadcast_to(a_lanes_re[None], (SUBLANES, STATE_LANES)),
            aim_b=jnp.broadcast_to(a_lanes_im[None], (SUBLANES, STATE_LANES)),
            bblk=_block_diag_b(bbar_re, bbar_im),
            cblk=_block_diag_c(ssm_c_re[l], ssm_c_im[l]),
            d_row=ssm_d[l].reshape(1, SSM_WIDTH),
            w_glu=w_glu[l].astype(BF16), b_glu=b_glu[l].reshape(1, SSM_WIDTH),
            w_out=w_out[l].astype(BF16),
        )
        h0_p = jnp.zeros((bp, STATE_LANES), F32)
        xp, k1, v1, f1, h1 = _layer(xp, mod3[:bp], h0_p, None, p)
        h0_s = _state_layout(state_ssm_re[l], state_ssm_im[l])
        xs, k2, v2, f2, h2 = _layer(xs, mod3[bp:], h0_s, (cache_k[l], cache_v[l], cache_logf[l]), p)
        outs_p.append((k1, v1, f1) + _state_unlayout(h1))
        outs_s.append((k2, v2, f2) + _state_unlayout(h2))

    def stack(items, i, shape_tail):
        arr = jnp.stack([it[i] for it in items])
        return arr.reshape(arr.shape[:3] + shape_tail) if shape_tail else arr

    hd_tail = (N_HEADS, HEAD_DIM)
    return (xp, xs,
            stack(outs_p, 0, hd_tail), stack(outs_p, 1, hd_tail), stack(outs_p, 2, ()),
            stack(outs_p, 3, ()), stack(outs_p, 4, ()),
            stack(outs_s, 0, hd_tail), stack(outs_s, 1, hd_tail), stack(outs_s, 2, ()),
            stack(outs_s, 3, ()), stack(outs_s, 4, ()))
```

```python
import functools

import numpy as np
import jax
import jax.numpy as jnp
from jax import lax
from jax.experimental import pallas as pl
from jax.experimental.pallas import tpu as pltpu

F32 = jnp.float32
BF16 = jnp.bfloat16

N_HEADS = 8
HEAD_DIM = 64
ATTN_WIDTH = N_HEADS * HEAD_DIM
SSM_GROUPS = 32
SSM_GROUP = 16
SSM_STATE = 64
SSM_WIDTH = SSM_GROUPS * SSM_GROUP
STATE_LANES = 2 * SSM_GROUPS * SSM_STATE
HALF_LANES = STATE_LANES // 2
HALF_CH = SSM_WIDTH // 2
ATTN_SCALE = HEAD_DIM ** -0.5
NORM_EPS = 1e-6
LANES = 128
SUBLANES = 8
NEG_BIG = -1e30
LOG2E = float(np.log2(np.e))
GELU_C = float(np.float32(np.sqrt(2.0 / np.pi)))
VMEM_LIMIT = 56 * 1024 * 1024
SKIP_LOGIT_GAP = 105.0
BF16_NORM_SLACK = 1.02
S5_PHASES = 4
ATTN_SUB = 512
BOUNDED_LOGIT_RANGE = 96.0


def _cparams(sem):
    return pltpu.CompilerParams(dimension_semantics=sem, vmem_limit_bytes=VMEM_LIMIT)


def _sigmoid(x):
    return 1.0 / (1.0 + jnp.exp(-x))


def _silu(x):
    return x * _sigmoid(x)


def _split3(x):
    hi = x.astype(BF16)
    r1 = x - hi.astype(F32)
    mid = r1.astype(BF16)
    lo = (r1 - mid.astype(F32)).astype(BF16)
    return hi, mid, lo


def _tri(n):
    return jnp.tril(jnp.ones((n, n), F32)).astype(BF16)


def _mod_kernel(c_ref, w_ref, b_ref, o_ref):
    c = c_ref[...]
    o_ref[...] = jnp.dot(_silu(c), w_ref[...], preferred_element_type=F32) + b_ref[...]


def _mod(c, w_ada, b_ada):
    n, d = c.shape
    n_out = w_ada.shape[1]
    blk = 1024
    return pl.pallas_call(
        _mod_kernel,
        grid=(n_out // blk,),
        in_specs=[pl.BlockSpec((n, d), lambda j: (0, 0)),
                  pl.BlockSpec((d, blk), lambda j: (0, j)),
                  pl.BlockSpec((1, blk), lambda j: (0, j))],
        out_specs=pl.BlockSpec((n, blk), lambda j: (0, j)),
        out_shape=jax.ShapeDtypeStruct((n, n_out), F32),
        compiler_params=_cparams(("arbitrary",)),
        name="mod",
    )(c, w_ada, b_ada.reshape(1, n_out))


def _zoh_kernel(ldt_ref, are_ref, aim_ref, bre_ref, bim_ref,
                abre_ref, abim_ref, bbre_ref, bbim_ref):
    dt = jnp.exp(ldt_ref[...])
    a_re = are_ref[...]
    a_im = aim_ref[...]
    mag = jnp.exp(a_re * dt)
    ang = a_im * dt
    abar_re = mag * jnp.cos(ang)
    abar_im = mag * jnp.sin(ang)
    den = a_re * a_re + a_im * a_im
    n_re = abar_re - 1.0
    n_im = abar_im
    q_re = (n_re * a_re + n_im * a_im) / den
    q_im = (n_im * a_re - n_re * a_im) / den
    b_re = bre_ref[...]
    b_im = bim_ref[...]
    abre_ref[...] = abar_re
    abim_ref[...] = abar_im
    bbre_ref[...] = q_re * b_re - q_im * b_im
    bbim_ref[...] = q_re * b_im + q_im * b_re


def _zoh(log_dt, a_re, a_im, b_re, b_im):
    g, n, p = b_re.shape
    rows = g * n
    ldt = jnp.broadcast_to(log_dt[:, None], (g, n)).reshape(rows, 1)
    outs = pl.pallas_call(
        _zoh_kernel,
        out_shape=[jax.ShapeDtypeStruct((rows, 1), F32), jax.ShapeDtypeStruct((rows, 1), F32),
                   jax.ShapeDtypeStruct((rows, p), F32), jax.ShapeDtypeStruct((rows, p), F32)],
        name="zoh",
    )(ldt, a_re.reshape(rows, 1), a_im.reshape(rows, 1), b_re.reshape(rows, p), b_im.reshape(rows, p))
    abar_re, abar_im, bbar_re, bbar_im = outs
    return (abar_re.reshape(g, n), abar_im.reshape(g, n),
            bbar_re.reshape(g, n, p), bbar_im.reshape(g, n, p))


def _state_layout(x_re, x_im):
    lead = x_re.shape[:-2]
    hg = SSM_GROUPS // 2
    r = x_re.reshape(lead + (2, 1, hg * SSM_STATE))
    i = x_im.reshape(lead + (2, 1, hg * SSM_STATE))
    return jnp.concatenate([r, i], axis=-2).reshape(lead + (STATE_LANES,))


def _state_unlayout(h):
    lead = h.shape[:-1]
    hg = SSM_GROUPS // 2
    x = h.reshape(lead + (2, 2, hg, SSM_STATE))
    re = x[..., :, 0, :, :].reshape(lead + (SSM_GROUPS, SSM_STATE))
    im = x[..., :, 1, :, :].reshape(lead + (SSM_GROUPS, SSM_STATE))
    return re, im


def _block_diag_b(bbar_re, bbar_im):
    hg = SSM_GROUPS // 2
    eye = jnp.eye(hg, dtype=F32)
    out = []
    for hf in range(2):
        parts = []
        for bb in (bbar_re, bbar_im):
            b = bb[hf * hg:(hf + 1) * hg]
            m = jnp.einsum('gnp,gk->gpkn', b, eye)
            parts.append(m.reshape(hg * SSM_GROUP, hg * SSM_STATE))
        out.append(jnp.concatenate(parts, axis=1))
    return jnp.stack(out).astype(BF16)


def _block_diag_c(c_re, c_im):
    hg = SSM_GROUPS // 2
    eye = jnp.eye(hg, dtype=F32)
    out = []
    for hf in range(2):
        parts = []
        for cc in (c_re, -c_im):
            c = cc[hf * hg:(hf + 1) * hg]
            m = jnp.einsum('gpn,gk->gnkp', c, eye)
            parts.append(m.reshape(hg * SSM_STATE, hg * SSM_GROUP))
        out.append(jnp.concatenate(parts, axis=0))
    return jnp.stack(out).astype(BF16)


def _inproj_kernel(x_ref, mod_ref, g_ref, wm_ref, wf_ref, bf_ref, gq_ref, gk_ref, e_ref, tri_ref, cinit_ref,
                   u_ref, zs_ref, q_ref, kout_ref, vout_ref, katt_ref, vatt_ref, za_ref, logf_ref, cum_ref,
                   *rest):
    carry_ref = rest[-1]
    cum_rows_ref = rest[0] if len(rest) == 2 else None

    @pl.when(pl.program_id(1) == 0)
    def _():
        carry_ref[...] = cinit_ref[0]

    x = x_ref[0]
    ms = jnp.mean(x * x, axis=-1, keepdims=True)
    xn = x * lax.rsqrt(ms + NORM_EPS) * g_ref[...]
    shift = mod_ref[0, 0:1, :]
    scale = mod_ref[0, 1:2, :]
    hb = (xn * (1.0 + scale) + shift).astype(BF16)

    def sec(i):
        return jnp.dot(hb, wm_ref[:, i * 512:(i + 1) * 512], preferred_element_type=F32)

    def head_rmsnorm(y, g):
        msq = jnp.dot((y * y).astype(BF16), e_ref[...], preferred_element_type=F32)
        return y * lax.rsqrt(msq + NORM_EPS) * g

    u_ref[0] = sec(0)
    zs_ref[0] = sec(1).astype(BF16)
    q = head_rmsnorm(sec(2), gq_ref[...])
    q_ref[0] = (q * (ATTN_SCALE * LOG2E)).astype(BF16)
    k = head_rmsnorm(sec(3), gk_ref[...])
    kout_ref[0] = k
    v = sec(4)
    vout_ref[0] = v
    kb = k.astype(BF16)
    vb = v.astype(BF16)
    for h in range(N_HEADS):
        katt_ref[0, h] = kb[:, h * HEAD_DIM:(h + 1) * HEAD_DIM]
        vatt_ref[0, h] = vb[:, h * HEAD_DIM:(h + 1) * HEAD_DIM]
    za_ref[0] = sec(5).astype(BF16)
    f = jnp.dot(hb, wf_ref[...], preferred_element_type=F32) + bf_ref[...]
    logf = jnp.minimum(f, 0.0) - jnp.log1p(jnp.exp(-jnp.abs(f)))
    logf_ref[0] = logf[:, :N_HEADS]
    lane = lax.broadcasted_iota(jnp.int32, logf.shape, 1)
    hi, mid, lo = (part.astype(F32) for part in _split3(jnp.where(lane < N_HEADS, logf, 0.0)))
    packed = hi + pltpu.roll(mid, N_HEADS, 1) + pltpu.roll(lo, 2 * N_HEADS, 1)
    r = jnp.dot(tri_ref[...], packed.astype(BF16), preferred_element_type=F32)
    c = carry_ref[...] + r + pltpu.roll(r, LANES - N_HEADS, 1) + pltpu.roll(r, LANES - 2 * N_HEADS, 1)
    cum_ref[0] = c[:, :N_HEADS]
    if cum_rows_ref is not None:
        cum_rows_ref[0] = c.T[:N_HEADS, :]
    tm = c.shape[0]
    carry_ref[...] = c[tm - 1:tm, :]


def _inproj(x, mod3, norm_g, w_main, w_f, b_f, gq, gk, e_mat, cum_init, tm):
    b, t, d = x.shape
    nt = t // tm
    tok = lambda w, dt: jax.ShapeDtypeStruct((b, t, w), dt)
    heads = jax.ShapeDtypeStruct((b, N_HEADS, t, HEAD_DIM), BF16)
    const = lambda shape: pl.BlockSpec(shape, lambda i, j: (0,) * len(shape))
    tokspec = lambda w: pl.BlockSpec((1, tm, w), lambda i, j: (i, j, 0))
    headspec = pl.BlockSpec((1, N_HEADS, tm, HEAD_DIM), lambda i, j: (i, 0, j, 0))
    out_specs = [tokspec(SSM_WIDTH), tokspec(SSM_WIDTH), tokspec(ATTN_WIDTH), tokspec(ATTN_WIDTH),
                 tokspec(ATTN_WIDTH), headspec, headspec, tokspec(ATTN_WIDTH), tokspec(N_HEADS),
                 tokspec(N_HEADS)]
    out_shape = [tok(SSM_WIDTH, F32), tok(SSM_WIDTH, BF16), tok(ATTN_WIDTH, BF16), tok(ATTN_WIDTH, F32),
                 tok(ATTN_WIDTH, F32), heads, heads, tok(ATTN_WIDTH, BF16), tok(N_HEADS, F32),
                 tok(N_HEADS, F32)]
    if tm % LANES == 0:
        out_specs.append(pl.BlockSpec((1, N_HEADS, tm), lambda i, j: (i, 0, j)))
        out_shape.append(jax.ShapeDtypeStruct((b, N_HEADS, t), F32))
    outs = pl.pallas_call(
        _inproj_kernel,
        grid=(b, nt),
        in_specs=[tokspec(d),
                  pl.BlockSpec((1, 3, d), lambda i, j: (i, 0, 0)),
                  const((1, d)), const(w_main.shape), const(w_f.shape), const((1, LANES)),
                  const((1, ATTN_WIDTH)), const((1, ATTN_WIDTH)), const(e_mat.shape), const((tm, tm)),
                  pl.BlockSpec((1, 1, LANES), lambda i, j: (i, 0, 0))],
        out_specs=out_specs,
        out_shape=out_shape,
        scratch_shapes=[pltpu.VMEM((1, LANES), F32)],
        compiler_params=_cparams(("arbitrary", "arbitrary")),
        name="inproj",
    )(x, mod3, norm_g, w_main, w_f, b_f, gq, gk, e_mat, _tri(tm), cum_init)
    if len(outs) == 10:
        outs = list(outs) + [jnp.transpose(outs[9], (0, 2, 1))]
    return outs


def _cumsum_kernel(x_ref, tri_ref, o_ref, carry_ref):
    @pl.when(pl.program_id(1) == 0)
    def _():
        carry_ref[...] = jnp.zeros_like(carry_ref)

    c = carry_ref[...]
    for part in _split3(x_ref[0]):
        c = c + jnp.dot(tri_ref[...], part, preferred_element_type=F32)
    o_ref[0] = c
    tc = o_ref.shape[1]
    carry_ref[...] = c[tc - 1:tc, :]


def _cumsum(x, tc):
    b, t, w = x.shape
    return pl.pallas_call(
        _cumsum_kernel,
        grid=(b, t // tc),
        in_specs=[pl.BlockSpec((1, tc, w), lambda i, j: (i, j, 0)),
                  pl.BlockSpec((tc, tc), lambda i, j: (0, 0))],
        out_specs=pl.BlockSpec((1, tc, w), lambda i, j: (i, j, 0)),
        out_shape=jax.ShapeDtypeStruct((b, t, w), F32),
        scratch_shapes=[pltpu.VMEM((1, w), F32)],
        compiler_params=_cparams(("arbitrary", "arbitrary")),
        name="cumsum",
    )(x, _tri(tc))


def _s5_kernel(u_ref, h0_ref, are_ref, aim_ref, bblk_ref, cblk_ref, d_ref,
               y_ref, hout_ref,
               ut_ref, utm_ref, bu_ref, ytm_ref, *, tt, pitch):
    nb = u_ref.shape[0]
    nslab = SSM_WIDTH // LANES

    @pl.when(pl.program_id(0) == 0)
    def _():
        hout_ref[...] = h0_ref[...]

    for b in range(nb):
        for k in range(nslab):
            ut_ref[k, b * pitch:b * pitch + tt, :] = u_ref[b, :, k * LANES:(k + 1) * LANES]

    nph = S5_PHASES if tt % (S5_PHASES * SUBLANES) == 0 else 1
    tp = tt // nph
    chunk = 4 * LANES
    quarter = HALF_LANES // 2
    chunks = [(hf * HALF_LANES + c * chunk, hf * HALF_LANES + c * chunk + quarter)
              for hf in range(2) for c in range(quarter // chunk)]

    def b_proj(ph):
        for t in range(ph * tp, (ph + 1) * tp):
            for k in range(nslab):
                utm_ref[t * nb:(t + 1) * nb, k * LANES:(k + 1) * LANES] = ut_ref[k, pl.ds(t, nb, stride=pitch), :]
        rows = slice(ph * tp * nb, (ph + 1) * tp * nb)
        for hf in range(2):
            uh = utm_ref[rows, hf * HALF_CH:(hf + 1) * HALF_CH].astype(BF16)
            bu_ref[rows, hf * HALF_LANES:(hf + 1) * HALF_LANES] = jnp.dot(
                uh, bblk_ref[hf], preferred_element_type=F32)

    def scan(ph):
        state = [(hout_ref[:, lo_re:lo_re + chunk], hout_ref[:, lo_im:lo_im + chunk]) for lo_re, lo_im in chunks]
        for t in range(ph * tp, (ph + 1) * tp):
            r = slice(t * nb, (t + 1) * nb)
            for ci, (lo_re, lo_im) in enumerate(chunks):
                xr, xi = state[ci]
                ar = are_ref[:, lo_re:lo_re + chunk]
                ai = aim_ref[:, lo_re:lo_re + chunk]
                nr = ar * xr - ai * xi + bu_ref[r, lo_re:lo_re + chunk]
                ni = ar * xi + ai * xr + bu_ref[r, lo_im:lo_im + chunk]
                bu_ref[r, lo_re:lo_re + chunk] = nr
                bu_ref[r, lo_im:lo_im + chunk] = ni
                state[ci] = (nr, ni)
        for (lo_re, lo_im), (xr, xi) in zip(chunks, state):
            hout_ref[:, lo_re:lo_re + chunk] = xr
            hout_ref[:, lo_im:lo_im + chunk] = xi

    def c_proj(ph):
        rows = slice(ph * tp * nb, (ph + 1) * tp * nb)
        for hf in range(2):
            xh = bu_ref[rows, hf * HALF_LANES:(hf + 1) * HALF_LANES].astype(BF16)
            yh = jnp.dot(xh, cblk_ref[hf], preferred_element_type=F32)
            yh = yh + d_ref[:, hf * HALF_CH:(hf + 1) * HALF_CH] * utm_ref[rows, hf * HALF_CH:(hf + 1) * HALF_CH]
            for kk in range(HALF_CH // LANES):
                ytm_ref[hf * (HALF_CH // LANES) + kk, rows, :] = yh[:, kk * LANES:(kk + 1) * LANES]
        for b in range(nb):
            for k in range(nslab):
                y_ref[b, ph * tp:(ph + 1) * tp, k * LANES:(k + 1) * LANES] = (
                    ytm_ref[k, pl.ds(ph * tp * nb + b, tp, stride=nb), :])

    b_proj(0)
    for ph in range(nph):
        if ph + 1 < nph:
            b_proj(ph + 1)
        scan(ph)
        c_proj(ph)


def _s5(u, h0, are_b, aim_b, bblk, cblk, d_row, tt):
    b, t, w = u.shape
    assert b == SUBLANES and t % tt == 0 and tt % SUBLANES == 0
    pitch = tt + SUBLANES
    rows = tt * b
    const = lambda shape: pl.BlockSpec(shape, lambda j: (0,) * len(shape))
    return pl.pallas_call(
        functools.partial(_s5_kernel, tt=tt, pitch=pitch),
        grid=(t // tt,),
        in_specs=[pl.BlockSpec((b, tt, w), lambda j: (0, j, 0)),
                  const((b, STATE_LANES)), const((b, STATE_LANES)), const((b, STATE_LANES)),
                  const(bblk.shape), const(cblk.shape), const((1, w))],
        out_specs=[pl.BlockSpec((b, tt, w), lambda j: (0, j, 0)), const((b, STATE_LANES))],
        out_shape=[jax.ShapeDtypeStruct((b, t, w), F32), jax.ShapeDtypeStruct((b, STATE_LANES), F32)],
        scratch_shapes=[pltpu.VMEM((w // LANES, b * pitch, LANES), F32),
                        pltpu.VMEM((rows, w), F32),
                        pltpu.VMEM((rows, STATE_LANES), F32),
                        pltpu.VMEM((w // LANES, rows, LANES), F32)],
        compiler_params=_cparams(("arbitrary",)),
        name="s5",
    )(u, h0, are_b, aim_b, bblk, cblk, d_row)


def _attn_kernel(*refs, bq, has_past):
    if has_past:
        (q_ref, kc_ref, vc_ref, cq_ref, ckc_ref, cec_ref, gq_ref, gk_ref, za_ref,
         kp_ref, vp_ref, ckp_ref, cep_ref, o_ref) = refs
    else:
        q_ref, kc_ref, vc_ref, cq_ref, ckc_ref, cec_ref, gq_ref, gk_ref, za_ref, o_ref = refs
    hp = pl.program_id(1)
    heads_per_step = q_ref.shape[2] // HEAD_DIM
    heads = range(heads_per_step)
    nq = q_ref.shape[1] // bq

    smax = (jnp.max(jnp.abs(gq_ref[...]), axis=-1, keepdims=True)
            * jnp.max(jnp.abs(gk_ref[...]), axis=-1, keepdims=True)
            * (HEAD_DIM * ATTN_SCALE * BF16_NORM_SLACK * BF16_NORM_SLACK))
    skip_below = -(SKIP_LOGIT_GAP + 2.0 * smax)
    smax2 = smax * LOG2E

    def load_cur(ref, hh, start, n):
        return ref[0, hh, pl.ds(start, n), :]

    def load_past(ref, hh, start, n):
        return ref[0, pl.ds(start, n), hh * HEAD_DIM:(hh + 1) * HEAD_DIM].astype(BF16)

    cur = (load_cur, kc_ref, vc_ref, ckc_ref)
    past = (load_past, kp_ref, vp_ref, ckp_ref) if has_past else None

    def q_block(qi, online):
        row0 = pl.multiple_of(qi * bq, bq)
        qrows = pl.ds(row0, bq)
        cq_all = cq_ref[0, qrows, :]
        head_lane = lax.broadcasted_iota(jnp.int32, cq_all.shape, 1)
        qs = [q_ref[0, qrows, hh * HEAD_DIM:(hh + 1) * HEAD_DIM] for hh in heads]

        def cq_of(hh, rows):
            h = hp * heads_per_step + hh
            return jnp.sum(jnp.where(head_lane[rows] == h, cq_all[rows], 0.0), axis=-1, keepdims=True)

        def first_live(ce_ref, n_visible):
            dead = None
            for hh in heads:
                cq_first = cq_of(hh, slice(0, 1))
                ce = ce_ref[0, hh]
                idx = lax.broadcasted_iota(jnp.int32, ce.shape, 1)
                d = jnp.logical_and(cq_first - ce < skip_below, idx < n_visible)
                dead = d if dead is None else jnp.logical_and(dead, d)
            return jnp.sum(dead.astype(jnp.int32))

        def finish(outs):
            o = jnp.concatenate(outs, axis=-1)
            o_ref[0, qrows, :] = (o * _silu(za_ref[0, qrows, :].astype(F32))).astype(BF16)

        def sweep(block, src, lo, hi, carry):
            odd = lo + jnp.bitwise_and(hi - lo, 1)
            carry = lax.fori_loop(lo, odd, lambda j, c: block(src, j, c), carry)

            def pair(i, c):
                j = odd + 2 * i
                return block(src, j + 1, block(src, j, c))
            return lax.fori_loop(0, lax.shift_right_logical(hi - odd, 1), pair, carry)

        def all_blocks(block, carry):
            if has_past:
                npb = ckp_ref.shape[2]
                carry = sweep(block, past, first_live(cep_ref, npb), npb, carry)
            return sweep(block, cur, first_live(cec_ref, qi), qi, carry)

        if not online:
            refs_row = [smax2 - cq_of(hh, slice(None)) * LOG2E for hh in heads]

            def piece(src, hh, j, off, sub, causal, carry_h):
                load, k_ref, v_ref, ck_ref = src
                lp, acc = carry_h
                bk = ck_ref.shape[4]
                start = pl.multiple_of(j * bk + off, sub)
                s = lax.dot_general(qs[hh], load(k_ref, hh, start, sub), (((1,), (1,)), ((), ())),
                                    preferred_element_type=F32)
                t = s - ck_ref[0, hh, j][:, off:off + sub] * LOG2E - refs_row[hh]
                if causal:
                    r = lax.broadcasted_iota(jnp.int32, t.shape, 0)
                    c = lax.broadcasted_iota(jnp.int32, t.shape, 1) + off
                    t = jnp.where(r >= c, t, NEG_BIG)
                p = jnp.exp2(t)
                if sub < LANES:
                    psum = jnp.concatenate([p, jnp.zeros((p.shape[0], LANES - sub), F32)], axis=-1)
                else:
                    psum = p[:, 0:LANES]
                    for c0 in range(LANES, sub, LANES):
                        psum = psum + p[:, c0:c0 + LANES]
                pv = jnp.dot(p.astype(BF16), load(v_ref, hh, start, sub), preferred_element_type=F32)
                return lp + psum, acc + pv

            def block(src, j, carry):
                bk = src[3].shape[4]
                sub = min(bk, ATTN_SUB)
                new = []
                for hh in heads:
                    c = carry[hh]
                    for off in range(0, bk, sub):
                        c = piece(src, hh, j, off, sub, False, c)
                    new.append(c)
                return tuple(new)

            carry = tuple((jnp.zeros((bq, LANES), F32), jnp.zeros((bq, HEAD_DIM), F32)) for _ in heads)
            carry = all_blocks(block, carry)
            outs = []
            for hh in heads:
                lp, acc = piece(cur, hh, qi, 0, bq, True, carry[hh])
                outs.append(acc / jnp.sum(lp, axis=-1, keepdims=True))
            finish(outs)
        else:
            def update(s, v, carry_h):
                m, l, acc = carry_h
                m_new = jnp.maximum(m, jnp.max(s, axis=-1, keepdims=True))
                alpha = jnp.exp2(m - m_new)
                p = jnp.exp2(s - m_new)
                l = alpha * l + jnp.sum(p, axis=-1, keepdims=True)
                acc = alpha * acc + jnp.dot(p.astype(BF16), v, preferred_element_type=F32)
                return m_new, l, acc

            def scores(src, hh, j):
                load, k_ref, _, ck_ref = src
                bk = ck_ref.shape[4]
                start = pl.multiple_of(j * bk, bk)
                s = lax.dot_general(qs[hh], load(k_ref, hh, start, bk), (((1,), (1,)), ((), ())),
                                    preferred_element_type=F32)
                return s - ck_ref[0, hh, j] * LOG2E, start, bk

            def block(src, j, carry):
                new = []
                for hh in heads:
                    s, start, bk = scores(src, hh, j)
                    new.append(update(s, src[0](src[2], hh, start, bk), carry[hh]))
                return tuple(new)

            carry = tuple((jnp.full((bq, 1), NEG_BIG, F32), jnp.zeros((bq, 1), F32),
                           jnp.zeros((bq, HEAD_DIM), F32)) for _ in heads)
            carry = all_blocks(block, carry)
            row = lax.broadcasted_iota(jnp.int32, (bq, bq), 0)
            col = lax.broadcasted_iota(jnp.int32, (bq, bq), 1)
            outs = []
            for hh in heads:
                s, start, bk = scores(cur, hh, qi)
                s = jnp.where(row >= col, s, NEG_BIG)
                _, l, acc = update(s, load_cur(vc_ref, hh, start, bk), carry[hh])
                outs.append(acc / l)
            finish(outs)
        return 0

    small_logits = jnp.max(smax2) * 2.0 <= BOUNDED_LOGIT_RANGE

    @pl.when(small_logits)
    def _():
        lax.fori_loop(0, nq, lambda qi, _: q_block(qi, False), 0)

    @pl.when(jnp.logical_not(small_logits))
    def _():
        lax.fori_loop(0, nq, lambda qi, _: q_block(qi, True), 0)


def _row_blocks(cum_rows, blk):
    b, h, t = cum_rows.shape
    rows = cum_rows.reshape(b, h, t // blk, 1, blk)
    return rows, rows[:, :, :, 0, blk - 1].reshape(b, h, 1, t // blk)


def _attn(q, k_cur, v_cur, cum, cum_rows, za, gq, gk, bq, past=None):
    b, tq, _ = q.shape
    hps = LANES // HEAD_DIM
    nq = tq // bq
    ckc, cec = _row_blocks(cum_rows, bq)
    qspec = pl.BlockSpec((1, tq, LANES), lambda i, h: (i, 0, h))
    kvspec = pl.BlockSpec((1, hps, tq, HEAD_DIM), lambda i, h: (i, h, 0, 0))
    rowspec = lambda n, w: pl.BlockSpec((1, hps, n, 1, w), lambda i, h: (i, h, 0, 0, 0))
    endspec = lambda n: pl.BlockSpec((1, hps, 1, n), lambda i, h: (i, h, 0, 0))
    gspec = pl.BlockSpec((1, ATTN_WIDTH), lambda i, h: (0, 0))
    in_specs = [qspec, kvspec, kvspec, pl.BlockSpec((1, tq, N_HEADS), lambda i, h: (i, 0, 0)),
                rowspec(nq, bq), endspec(nq), gspec, gspec, qspec]
    args = [q, k_cur, v_cur, cum, ckc, cec, gq, gk, za]
    if past is not None:
        k_past, v_past, cum_past_rows, bkp = past
        tp = k_past.shape[1]
        ckp, cep = _row_blocks(cum_past_rows, bkp)
        pastspec = pl.BlockSpec((1, tp, LANES), lambda i, h: (i, 0, h))
        in_specs += [pastspec, pastspec, rowspec(tp // bkp, bkp), endspec(tp // bkp)]
        args += [k_past, v_past, ckp, cep]
    return pl.pallas_call(
        functools.partial(_attn_kernel, bq=bq, has_past=past is not None),
        grid=(b, N_HEADS // hps),
        in_specs=in_specs,
        out_specs=qspec,
        out_shape=jax.ShapeDtypeStruct((b, tq, ATTN_WIDTH), BF16),
        compiler_params=_cparams(("arbitrary", "arbitrary")),
        name="attn",
    )(*args)


def _out_kernel(x_ref, mod_ref, ys_ref, zs_ref, ya_ref, wglu_ref, bglu_ref, wo_ref, o_ref):
    y = ys_ref[0]
    g = y * (0.5 * (1.0 + jnp.tanh(GELU_C * (y + 0.044715 * (y * y * y)))))
    glu = jnp.dot(g.astype(BF16), wglu_ref[...], preferred_element_type=F32) + bglu_ref[...]
    y1 = g * _sigmoid(glu) * _silu(zs_ref[0].astype(F32))
    mixed = (jnp.dot(y1.astype(BF16), wo_ref[0:SSM_WIDTH, :], preferred_element_type=F32)
             + jnp.dot(ya_ref[0], wo_ref[SSM_WIDTH:, :], preferred_element_type=F32))
    o_ref[0] = x_ref[0] + mod_ref[0, 2:3, :] * mixed


def _out(x, mod3, y_ssm, zs, y_att, w_glu, b_glu, w_out, tm):
    b, t, d = x.shape
    const = lambda shape: pl.BlockSpec(shape, lambda i, j: (0,) * len(shape))
    tokspec = lambda w: pl.BlockSpec((1, tm, w), lambda i, j: (i, j, 0))
    return pl.pallas_call(
        _out_kernel,
        grid=(b, t // tm),
        in_specs=[tokspec(d), pl.BlockSpec((1, 3, d), lambda i, j: (i, 0, 0)),
                  tokspec(SSM_WIDTH), tokspec(SSM_WIDTH), tokspec(ATTN_WIDTH),
                  const(w_glu.shape), const((1, SSM_WIDTH)), const(w_out.shape)],
        out_specs=tokspec(d),
        out_shape=jax.ShapeDtypeStruct((b, t, d), F32),
        compiler_params=_cparams(("arbitrary", "arbitrary")),
        name="out",
    )(x, mod3, y_ssm, zs, y_att, w_glu, b_glu, w_out)


def _tile(t, pref):
    return pref if t % pref == 0 else t


def _layer(x, mod3, h0, past, p):
    b, t, d = x.shape
    tm = _tile(t, 512)
    if past is None:
        cum_init = jnp.zeros((b, 1, LANES), F32)
        attn_past = None
    else:
        ck_past, cv_past, clogf = past
        tp = ck_past.shape[1]
        clogf_pad = jnp.pad(clogf.astype(F32), ((0, 0), (0, 0), (0, LANES - N_HEADS)))
        cum_past = _cumsum(clogf_pad, _tile(tp, 512))
        cum_init = cum_past[:, tp - 1:tp, :]
        attn_past = (ck_past.reshape(b, tp, ATTN_WIDTH), cv_past.reshape(b, tp, ATTN_WIDTH),
                     jnp.transpose(cum_past[:, :, :N_HEADS], (0, 2, 1)), _tile(tp, 512))

    u, zs, q, k_out, v_out, k_att, v_att, za, logf, cum, cum_rows = _inproj(
        x, mod3, p['norm_g'], p['w_main'], p['w_f'], p['b_f'], p['gq'], p['gk'], p['e_mat'], cum_init, tm)
    y_ssm, h_new = _s5(u, h0, p['are_b'], p['aim_b'], p['bblk'], p['cblk'], p['d_row'], _tile(t, 128))
    y_att = _attn(q, k_att, v_att, cum, cum_rows, za, p['gq'], p['gk'], _tile(t, 512), attn_past)
    y = _out(x, mod3, y_ssm, zs, y_att, p['w_glu'], p['b_glu'], p['w_out'], tm)
    return y, k_out, v_out, logf, h_new


def kernel(x_prompt, x_sample, cache_k, cache_v, cache_logf, state_ssm_re, state_ssm_im, c_prompt, c_sample,
           w_ada, b_ada, norm_g, w_in, b_f, q_norm_g, k_norm_g, ssm_log_dt, ssm_a_re, ssm_a_im,
           ssm_b_re, ssm_b_im, ssm_c_re, ssm_c_im, ssm_d, w_glu, b_glu, w_out):
    depth = w_ada.shape[0]
    d = x_prompt.shape[-1]
    bp = x_prompt.shape[0]
    bs = x_sample.shape[0]
    xp, xs = x_prompt, x_sample
    outs_p, outs_s = [], []
    for l in range(depth):
        mod = _mod(jnp.concatenate([c_prompt, c_sample], axis=0), w_ada[l], b_ada[l])
        mod3 = mod.reshape(bp + bs, 3, d)
        abar_re, abar_im, bbar_re, bbar_im = _zoh(ssm_log_dt[l], ssm_a_re[l], ssm_a_im[l],
                                                  ssm_b_re[l], ssm_b_im[l])
        a_lanes_re = _state_layout(abar_re, abar_re)
        a_lanes_im = _state_layout(abar_im, abar_im)
        split = 2 * SSM_WIDTH + 4 * ATTN_WIDTH
        hd = jnp.arange(ATTN_WIDTH) // HEAD_DIM
        p = dict(
            norm_g=norm_g[l].reshape(1, d),
            w_main=w_in[l][:, :split].astype(BF16),
            w_f=jnp.pad(w_in[l][:, split:], ((0, 0), (0, LANES - N_HEADS))).astype(BF16),
            b_f=jnp.pad(b_f[l], (0, LANES - N_HEADS)).reshape(1, LANES),
            gq=jnp.tile(q_norm_g[l], N_HEADS).reshape(1, ATTN_WIDTH),
            gk=jnp.tile(k_norm_g[l], N_HEADS).reshape(1, ATTN_WIDTH),
            e_mat=(hd[:, None] == hd[None, :]).astype(BF16) * (1.0 / HEAD_DIM),
            are_b=jnp.broadcast_to(a_lanes_re[None], (SUBLANES, STATE_LANES)),
            aim_b=jnp.broadcast_to(a_lanes_im[None], (SUBLANES, STATE_LANES)),
            bblk=_block_diag_b(bbar_re, bbar_im),
            cblk=_block_diag_c(ssm_c_re[l], ssm_c_im[l]),
            d_row=ssm_d[l].reshape(1, SSM_WIDTH),
            w_glu=w_glu[l].astype(BF16), b_glu=b_glu[l].reshape(1, SSM_WIDTH),
            w_out=w_out[l].astype(BF16),
        )
        h0_p = jnp.zeros((bp, STATE_LANES), F32)
        xp, k1, v1, f1, h1 = _layer(xp, mod3[:bp], h0_p, None, p)
        h0_s = _state_layout(state_ssm_re[l], state_ssm_im[l])
        xs, k2, v2, f2, h2 = _layer(xs, mod3[bp:], h0_s, (cache_k[l], cache_v[l], cache_logf[l]), p)
        outs_p.append((k1, v1, f1) + _state_unlayout(h1))
        outs_s.append((k2, v2, f2) + _state_unlayout(h2))

    def stack(items, i, shape_tail):
        arr = jnp.stack([it[i] for it in items])
        return arr.reshape(arr.shape[:3] + shape_tail) if shape_tail else arr

    hd_tail = (N_HEADS, HEAD_DIM)
    return (xp, xs,
            stack(outs_p, 0, hd_tail), stack(outs_p, 1, hd_tail), stack(outs_p, 2, ()),
            stack(outs_p, 3, ()), stack(outs_p, 4, ()),
            stack(outs_s, 0, hd_tail), stack(outs_s, 1, hd_tail), stack(outs_s, 2, ()),
            stack(outs_s, 3, ()), stack(outs_s, 4, ()))
```

```python
import functools

import numpy as np
import jax
import jax.numpy as jnp
from jax import lax
from jax.experimental import pallas as pl
from jax.experimental.pallas import tpu as pltpu

F32 = jnp.float32
BF16 = jnp.bfloat16

N_HEADS = 8
HEAD_DIM = 64
ATTN_WIDTH = N_HEADS * HEAD_DIM
SSM_GROUPS = 32
SSM_GROUP = 16
SSM_STATE = 64
SSM_WIDTH = SSM_GROUPS * SSM_GROUP
STATE_LANES = 2 * SSM_GROUPS * SSM_STATE
HALF_LANES = STATE_LANES // 2
HALF_CH = SSM_WIDTH // 2
ATTN_SCALE = HEAD_DIM ** -0.5
NORM_EPS = 1e-6
LANES = 128
SUBLANES = 8
NEG_BIG = -1e30
LOG2E = float(np.log2(np.e))
GELU_C = float(np.float32(np.sqrt(2.0 / np.pi)))
VMEM_LIMIT = 56 * 1024 * 1024
SKIP_LOGIT_GAP = 105.0
BF16_NORM_SLACK = 1.02
S5_PHASES = 2
ATTN_SUB = 512
BOUNDED_LOGIT_RANGE = 96.0


def _cparams(sem):
    return pltpu.CompilerParams(dimension_semantics=sem, vmem_limit_bytes=VMEM_LIMIT)


def _sigmoid(x):
    return 1.0 / (1.0 + jnp.exp(-x))


def _silu(x):
    return x * _sigmoid(x)


def _split3(x):
    hi = x.astype(BF16)
    r1 = x - hi.astype(F32)
    mid = r1.astype(BF16)
    lo = (r1 - mid.astype(F32)).astype(BF16)
    return hi, mid, lo


def _tri(n):
    return jnp.tril(jnp.ones((n, n), F32)).astype(BF16)


def _mod_kernel(c_ref, w_ref, b_ref, o_ref):
    c = c_ref[...]
    o_ref[...] = jnp.dot(_silu(c), w_ref[...], preferred_element_type=F32) + b_ref[...]


def _mod(c, w_ada, b_ada):
    n, d = c.shape
    n_out = w_ada.shape[1]
    blk = 1024
    return pl.pallas_call(
        _mod_kernel,
        grid=(n_out // blk,),
        in_specs=[pl.BlockSpec((n, d), lambda j: (0, 0)),
                  pl.BlockSpec((d, blk), lambda j: (0, j)),
                  pl.BlockSpec((1, blk), lambda j: (0, j))],
        out_specs=pl.BlockSpec((n, blk), lambda j: (0, j)),
        out_shape=jax.ShapeDtypeStruct((n, n_out), F32),
        compiler_params=_cparams(("arbitrary",)),
        name="mod",
    )(c, w_ada, b_ada.reshape(1, n_out))


def _zoh_kernel(ldt_ref, are_ref, aim_ref, bre_ref, bim_ref,
                abre_ref, abim_ref, bbre_ref, bbim_ref):
    dt = jnp.exp(ldt_ref[...])
    a_re = are_ref[...]
    a_im = aim_ref[...]
    mag = jnp.exp(a_re * dt)
    ang = a_im * dt
    abar_re = mag * jnp.cos(ang)
    abar_im = mag * jnp.sin(ang)
    den = a_re * a_re + a_im * a_im
    n_re = abar_re - 1.0
    n_im = abar_im
    q_re = (n_re * a_re + n_im * a_im) / den
    q_im = (n_im * a_re - n_re * a_im) / den
    b_re = bre_ref[...]
    b_im = bim_ref[...]
    abre_ref[...] = abar_re
    abim_ref[...] = abar_im
    bbre_ref[...] = q_re * b_re - q_im * b_im
    bbim_ref[...] = q_re * b_im + q_im * b_re


def _zoh(log_dt, a_re, a_im, b_re, b_im):
    g, n, p = b_re.shape
    rows = g * n
    ldt = jnp.broadcast_to(log_dt[:, None], (g, n)).reshape(rows, 1)
    outs = pl.pallas_call(
        _zoh_kernel,
        out_shape=[jax.ShapeDtypeStruct((rows, 1), F32), jax.ShapeDtypeStruct((rows, 1), F32),
                   jax.ShapeDtypeStruct((rows, p), F32), jax.ShapeDtypeStruct((rows, p), F32)],
        name="zoh",
    )(ldt, a_re.reshape(rows, 1), a_im.reshape(rows, 1), b_re.reshape(rows, p), b_im.reshape(rows, p))
    abar_re, abar_im, bbar_re, bbar_im = outs
    return (abar_re.reshape(g, n), abar_im.reshape(g, n),
            bbar_re.reshape(g, n, p), bbar_im.reshape(g, n, p))


def _state_layout(x_re, x_im):
    lead = x_re.shape[:-2]
    hg = SSM_GROUPS // 2
    r = x_re.reshape(lead + (2, 1, hg * SSM_STATE))
    i = x_im.reshape(lead + (2, 1, hg * SSM_STATE))
    return jnp.concatenate([r, i], axis=-2).reshape(lead + (STATE_LANES,))


def _state_unlayout(h):
    lead = h.shape[:-1]
    hg = SSM_GROUPS // 2
    x = h.reshape(lead + (2, 2, hg, SSM_STATE))
    re = x[..., :, 0, :, :].reshape(lead + (SSM_GROUPS, SSM_STATE))
    im = x[..., :, 1, :, :].reshape(lead + (SSM_GROUPS, SSM_STATE))
    return re, im


def _block_diag_b(bbar_re, bbar_im):
    hg = SSM_GROUPS // 2
    eye = jnp.eye(hg, dtype=F32)
    out = []
    for hf in range(2):
        parts = []
        for bb in (bbar_re, bbar_im):
            b = bb[hf * hg:(hf + 1) * hg]
            m = jnp.einsum('gnp,gk->gpkn', b, eye)
            parts.append(m.reshape(hg * SSM_GROUP, hg * SSM_STATE))
        out.append(jnp.concatenate(parts, axis=1))
    return jnp.stack(out).astype(BF16)


def _block_diag_c(c_re, c_im):
    hg = SSM_GROUPS // 2
    eye = jnp.eye(hg, dtype=F32)
    out = []
    for hf in range(2):
        parts = []
        for cc in (c_re, -c_im):
            c = cc[hf * hg:(hf + 1) * hg]
            m = jnp.einsum('gpn,gk->gnkp', c, eye)
            parts.append(m.reshape(hg * SSM_STATE, hg * SSM_GROUP))
        out.append(jnp.concatenate(parts, axis=0))
    return jnp.stack(out).astype(BF16)


def _inproj_kernel(x_ref, mod_ref, g_ref, wm_ref, wf_ref, bf_ref, gq_ref, gk_ref, e_ref, tri_ref, cinit_ref,
                   u_ref, zs_ref, q_ref, kout_ref, vout_ref, katt_ref, vatt_ref, za_ref, cum_ref, logf_ref,
                   *rest, rows):
    if rows:
        cum_rows_ref, carry_ref = rest
    else:
        carry_ref, = rest

    @pl.when(pl.program_id(1) == 0)
    def _():
        carry_ref[...] = cinit_ref[0]

    x = x_ref[0]
    ms = jnp.mean(x * x, axis=-1, keepdims=True)
    xn = x * lax.rsqrt(ms + NORM_EPS) * g_ref[...]
    shift = mod_ref[0, 0:1, :]
    scale = mod_ref[0, 1:2, :]
    hb = (xn * (1.0 + scale) + shift).astype(BF16)

    def sec(i):
        return jnp.dot(hb, wm_ref[:, i * 512:(i + 1) * 512], preferred_element_type=F32)

    def head_rmsnorm(y, g):
        msq = jnp.dot((y * y).astype(BF16), e_ref[...], preferred_element_type=F32)
        return y * lax.rsqrt(msq + NORM_EPS) * g

    u_ref[0] = sec(0)
    zs_ref[0] = sec(1).astype(BF16)
    q = head_rmsnorm(sec(2), gq_ref[...])
    q_ref[0] = (q * (ATTN_SCALE * LOG2E)).astype(BF16)
    k = head_rmsnorm(sec(3), gk_ref[...])
    kout_ref[0] = k
    v = sec(4)
    vout_ref[0] = v
    kb = k.astype(BF16)
    vb = v.astype(BF16)
    for h in range(N_HEADS):
        katt_ref[0, h] = kb[:, h * HEAD_DIM:(h + 1) * HEAD_DIM]
        vatt_ref[0, h] = vb[:, h * HEAD_DIM:(h + 1) * HEAD_DIM]
    za_ref[0] = sec(5).astype(BF16)
    f = jnp.dot(hb, wf_ref[...], preferred_element_type=F32) + bf_ref[...]
    logf = jnp.minimum(f, 0.0) - jnp.log1p(jnp.exp(-jnp.abs(f)))
    if rows:
        logf_ref[0] = logf.T[:N_HEADS, :]
    else:
        logf_ref[0] = logf[:, :N_HEADS]
    lane = lax.broadcasted_iota(jnp.int32, logf.shape, 1)
    hi, mid, lo = (part.astype(F32) for part in _split3(jnp.where(lane < N_HEADS, logf, 0.0)))
    packed = hi + pltpu.roll(mid, N_HEADS, 1) + pltpu.roll(lo, 2 * N_HEADS, 1)
    r = jnp.dot(tri_ref[...], packed.astype(BF16), preferred_element_type=F32)
    c = carry_ref[...] + r + pltpu.roll(r, LANES - N_HEADS, 1) + pltpu.roll(r, LANES - 2 * N_HEADS, 1)
    cum_ref[0] = c[:, :N_HEADS]
    if rows:
        cum_rows_ref[0] = c.T[:N_HEADS, :]
    tm = c.shape[0]
    carry_ref[...] = c[tm - 1:tm, :]


def _inproj(x, mod3, norm_g, w_main, w_f, b_f, gq, gk, e_mat, cum_init, tm):
    b, t, d = x.shape
    nt = t // tm
    tok = lambda w, dt: jax.ShapeDtypeStruct((b, t, w), dt)
    heads = jax.ShapeDtypeStruct((b, N_HEADS, t, HEAD_DIM), BF16)
    const = lambda shape: pl.BlockSpec(shape, lambda i, j: (0,) * len(shape))
    tokspec = lambda w: pl.BlockSpec((1, tm, w), lambda i, j: (i, j, 0))
    headspec = pl.BlockSpec((1, N_HEADS, tm, HEAD_DIM), lambda i, j: (i, 0, j, 0))
    rows = tm % LANES == 0
    rowspec = pl.BlockSpec((1, N_HEADS, tm), lambda i, j: (i, 0, j))
    rowshape = jax.ShapeDtypeStruct((b, N_HEADS, t), F32)
    out_specs = [tokspec(SSM_WIDTH), tokspec(SSM_WIDTH), tokspec(ATTN_WIDTH), tokspec(ATTN_WIDTH),
                 tokspec(ATTN_WIDTH), headspec, headspec, tokspec(ATTN_WIDTH), tokspec(N_HEADS)]
    out_shape = [tok(SSM_WIDTH, F32), tok(SSM_WIDTH, BF16), tok(ATTN_WIDTH, BF16), tok(ATTN_WIDTH, F32),
                 tok(ATTN_WIDTH, F32), heads, heads, tok(ATTN_WIDTH, BF16), tok(N_HEADS, F32)]
    out_specs += [rowspec, rowspec] if rows else [tokspec(N_HEADS)]
    out_shape += [rowshape, rowshape] if rows else [tok(N_HEADS, F32)]
    outs = pl.pallas_call(
        functools.partial(_inproj_kernel, rows=rows),
        grid=(b, nt),
        in_specs=[tokspec(d),
                  pl.BlockSpec((1, 3, d), lambda i, j: (i, 0, 0)),
                  const((1, d)), const(w_main.shape), const(w_f.shape), const((1, LANES)),
                  const((1, ATTN_WIDTH)), const((1, ATTN_WIDTH)), const(e_mat.shape), const((tm, tm)),
                  pl.BlockSpec((1, 1, LANES), lambda i, j: (i, 0, 0))],
        out_specs=out_specs,
        out_shape=out_shape,
        scratch_shapes=[pltpu.VMEM((1, LANES), F32)],
        compiler_params=_cparams(("arbitrary", "arbitrary")),
        name="inproj",
    )(x, mod3, norm_g, w_main, w_f, b_f, gq, gk, e_mat, _tri(tm), cum_init)
    main, cum = outs[:8], outs[8]
    if rows:
        logf_rows, cum_rows = outs[9:]
        logf = jnp.transpose(logf_rows, (0, 2, 1))
    else:
        logf = outs[9]
        cum_rows = jnp.transpose(cum, (0, 2, 1))
    return (*main, logf, cum, cum_rows)


def _cumsum_rows_kernel(x_ref, triu_ref, o_ref):
    tc = triu_ref.shape[0]
    total = jnp.zeros((x_ref.shape[1], 1), F32)
    for c0 in range(0, x_ref.shape[2], tc):
        c = total
        for part in _split3(x_ref[0, :, c0:c0 + tc]):
            c = c + jnp.dot(part, triu_ref[...], preferred_element_type=F32)
        o_ref[0, :, c0:c0 + tc] = c
        total = c[:, tc - 1:tc]


def _cumsum_rows(x, tc):
    b, h, t = x.shape
    return pl.pallas_call(
        _cumsum_rows_kernel,
        grid=(b,),
        in_specs=[pl.BlockSpec((1, h, t), lambda i: (i, 0, 0)),
                  pl.BlockSpec((tc, tc), lambda i: (0, 0))],
        out_specs=pl.BlockSpec((1, h, t), lambda i: (i, 0, 0)),
        out_shape=jax.ShapeDtypeStruct((b, h, t), F32),
        compiler_params=_cparams(("arbitrary",)),
        name="cumsum",
    )(x, jnp.triu(jnp.ones((tc, tc), F32)).astype(BF16))


def _s5_kernel(u_ref, h0_ref, are_ref, aim_ref, bblk_ref, cblk_ref, d_ref,
               y_ref, hout_ref,
               ut_ref, utm_ref, bu_ref, ytm_ref, *, tt, pitch):
    nb = u_ref.shape[0]
    nslab = SSM_WIDTH // LANES

    @pl.when(pl.program_id(0) == 0)
    def _():
        hout_ref[...] = h0_ref[...]

    for b in range(nb):
        for k in range(nslab):
            ut_ref[k, b * pitch:b * pitch + tt, :] = u_ref[b, :, k * LANES:(k + 1) * LANES]

    nph = S5_PHASES if tt % (S5_PHASES * SUBLANES) == 0 else 1
    tp = tt // nph
    chunk = 4 * LANES
    quarter = HALF_LANES // 2
    chunks = [(hf * HALF_LANES + c * chunk, hf * HALF_LANES + c * chunk + quarter)
              for hf in range(2) for c in range(quarter // chunk)]

    def b_proj(ph):
        for t in range(ph * tp, (ph + 1) * tp):
            for k in range(nslab):
                utm_ref[t * nb:(t + 1) * nb, k * LANES:(k + 1) * LANES] = ut_ref[k, pl.ds(t, nb, stride=pitch), :]
        rows = slice(ph * tp * nb, (ph + 1) * tp * nb)
        for hf in range(2):
            uh = utm_ref[rows, hf * HALF_CH:(hf + 1) * HALF_CH].astype(BF16)
            bu_ref[rows, hf * HALF_LANES:(hf + 1) * HALF_LANES] = jnp.dot(
                uh, bblk_ref[hf], preferred_element_type=F32)

    def scan(ph):
        state = [(hout_ref[:, lo_re:lo_re + chunk], hout_ref[:, lo_im:lo_im + chunk]) for lo_re, lo_im in chunks]
        for t in range(ph * tp, (ph + 1) * tp):
            r = slice(t * nb, (t + 1) * nb)
            for ci, (lo_re, lo_im) in enumerate(chunks):
                xr, xi = state[ci]
                ar = are_ref[:, lo_re:lo_re + chunk]
                ai = aim_ref[:, lo_re:lo_re + chunk]
                nr = ar * xr - ai * xi + bu_ref[r, lo_re:lo_re + chunk]
                ni = ar * xi + ai * xr + bu_ref[r, lo_im:lo_im + chunk]
                bu_ref[r, lo_re:lo_re + chunk] = nr
                bu_ref[r, lo_im:lo_im + chunk] = ni
                state[ci] = (nr, ni)
        for (lo_re, lo_im), (xr, xi) in zip(chunks, state):
            hout_ref[:, lo_re:lo_re + chunk] = xr
            hout_ref[:, lo_im:lo_im + chunk] = xi

    def c_proj(ph):
        rows = slice(ph * tp * nb, (ph + 1) * tp * nb)
        for hf in range(2):
            xh = bu_ref[rows, hf * HALF_LANES:(hf + 1) * HALF_LANES].astype(BF16)
            yh = jnp.dot(xh, cblk_ref[hf], preferred_element_type=F32)
            yh = yh + d_ref[:, hf * HALF_CH:(hf + 1) * HALF_CH] * utm_ref[rows, hf * HALF_CH:(hf + 1) * HALF_CH]
            for kk in range(HALF_CH // LANES):
                ytm_ref[hf * (HALF_CH // LANES) + kk, rows, :] = yh[:, kk * LANES:(kk + 1) * LANES]
        for b in range(nb):
            for k in range(nslab):
                y_ref[b, ph * tp:(ph + 1) * tp, k * LANES:(k + 1) * LANES] = (
                    ytm_ref[k, pl.ds(ph * tp * nb + b, tp, stride=nb), :])

    b_proj(0)
    for ph in range(nph):
        if ph + 1 < nph:
            b_proj(ph + 1)
        scan(ph)
        c_proj(ph)


def _s5(u, h0, are_b, aim_b, bblk, cblk, d_row, tt):
    b, t, w = u.shape
    assert b == SUBLANES and t % tt == 0 and tt % SUBLANES == 0
    pitch = tt + SUBLANES
    rows = tt * b
    const = lambda shape: pl.BlockSpec(shape, lambda j: (0,) * len(shape))
    return pl.pallas_call(
        functools.partial(_s5_kernel, tt=tt, pitch=pitch),
        grid=(t // tt,),
        in_specs=[pl.BlockSpec((b, tt, w), lambda j: (0, j, 0)),
                  const((b, STATE_LANES)), const((b, STATE_LANES)), const((b, STATE_LANES)),
                  const(bblk.shape), const(cblk.shape), const((1, w))],
        out_specs=[pl.BlockSpec((b, tt, w), lambda j: (0, j, 0)), const((b, STATE_LANES))],
        out_shape=[jax.ShapeDtypeStruct((b, t, w), F32), jax.ShapeDtypeStruct((b, STATE_LANES), F32)],
        scratch_shapes=[pltpu.VMEM((w // LANES, b * pitch, LANES), F32),
                        pltpu.VMEM((rows, w), F32),
                        pltpu.VMEM((rows, STATE_LANES), F32),
                        pltpu.VMEM((w // LANES, rows, LANES), F32)],
        compiler_params=_cparams(("arbitrary",)),
        name="s5",
    )(u, h0, are_b, aim_b, bblk, cblk, d_row)


def _attn_kernel(*refs, bq, has_past):
    if has_past:
        (q_ref, kc_ref, vc_ref, cq_ref, ckc_ref, cec_ref, gq_ref, gk_ref, za_ref,
         kp_ref, vp_ref, ckp_ref, cep_ref, o_ref) = refs
    else:
        q_ref, kc_ref, vc_ref, cq_ref, ckc_ref, cec_ref, gq_ref, gk_ref, za_ref, o_ref = refs
    hp = pl.program_id(1)
    heads_per_step = q_ref.shape[2] // HEAD_DIM
    heads = range(heads_per_step)
    nq = q_ref.shape[1] // bq

    smax = (jnp.max(jnp.abs(gq_ref[...]), axis=-1, keepdims=True)
            * jnp.max(jnp.abs(gk_ref[...]), axis=-1, keepdims=True)
            * (HEAD_DIM * ATTN_SCALE * BF16_NORM_SLACK * BF16_NORM_SLACK))
    skip_below = -(SKIP_LOGIT_GAP + 2.0 * smax)
    smax2 = smax * LOG2E

    def load_cur(ref, hh, start, n):
        return ref[0, hh, pl.ds(start, n), :]

    def load_past(ref, hh, start, n):
        return ref[0, pl.ds(start, n), hh * HEAD_DIM:(hh + 1) * HEAD_DIM].astype(BF16)

    cur = (load_cur, kc_ref, vc_ref, ckc_ref)
    past = (load_past, kp_ref, vp_ref, ckp_ref) if has_past else None

    def q_block(qi, online):
        row0 = pl.multiple_of(qi * bq, bq)
        qrows = pl.ds(row0, bq)
        cq_all = cq_ref[0, qrows, :]
        head_lane = lax.broadcasted_iota(jnp.int32, cq_all.shape, 1)
        qs = [q_ref[0, qrows, hh * HEAD_DIM:(hh + 1) * HEAD_DIM] for hh in heads]

        def cq_of(hh, rows):
            h = hp * heads_per_step + hh
            return jnp.sum(jnp.where(head_lane[rows] == h, cq_all[rows], 0.0), axis=-1, keepdims=True)

        def first_live(ce_ref, n_visible):
            dead = None
            for hh in heads:
                cq_first = cq_of(hh, slice(0, 1))
                ce = ce_ref[0, hh]
                idx = lax.broadcasted_iota(jnp.int32, ce.shape, 1)
                d = jnp.logical_and(cq_first - ce < skip_below, idx < n_visible)
                dead = d if dead is None else jnp.logical_and(dead, d)
            return jnp.sum(dead.astype(jnp.int32))

        def finish(outs):
            o = jnp.concatenate(outs, axis=-1)
            o_ref[0, qrows, :] = (o * _silu(za_ref[0, qrows, :].astype(F32))).astype(BF16)

        def sweep(block, src, lo, hi, carry):
            odd = lo + jnp.bitwise_and(hi - lo, 1)
            carry = lax.fori_loop(lo, odd, lambda j, c: block(src, j, c), carry)

            def pair(i, c):
                j = odd + 2 * i
                return block(src, j + 1, block(src, j, c))
            return lax.fori_loop(0, lax.shift_right_logical(hi - odd, 1), pair, carry)

        def all_blocks(block, carry):
            if has_past:
                npb = ckp_ref.shape[2]
                carry = sweep(block, past, first_live(cep_ref, npb), npb, carry)
            return sweep(block, cur, first_live(cec_ref, qi), qi, carry)

        if not online:
            refs_row = [smax2 - cq_of(hh, slice(None)) * LOG2E for hh in heads]

            def piece(src, hh, j, off, sub, causal, carry_h):
                load, k_ref, v_ref, ck_ref = src
                lp, acc = carry_h
                bk = ck_ref.shape[4]
                start = pl.multiple_of(j * bk + off, sub)
                s = lax.dot_general(qs[hh], load(k_ref, hh, start, sub), (((1,), (1,)), ((), ())),
                                    preferred_element_type=F32)
                t = s - ck_ref[0, hh, j][:, off:off + sub] * LOG2E - refs_row[hh]
                if causal:
                    r = lax.broadcasted_iota(jnp.int32, t.shape, 0)
                    c = lax.broadcasted_iota(jnp.int32, t.shape, 1) + off
                    t = jnp.where(r >= c, t, NEG_BIG)
                p = jnp.exp2(t)
                if sub < LANES:
                    psum = jnp.concatenate([p, jnp.zeros((p.shape[0], LANES - sub), F32)], axis=-1)
                else:
                    psum = p[:, 0:LANES]
                    for c0 in range(LANES, sub, LANES):
                        psum = psum + p[:, c0:c0 + LANES]
                pv = jnp.dot(p.astype(BF16), load(v_ref, hh, start, sub), preferred_element_type=F32)
                return lp + psum, acc + pv

            def block(src, j, carry):
                bk = src[3].shape[4]
                sub = min(bk, ATTN_SUB)
                new = []
                for hh in heads:
                    c = carry[hh]
                    for off in range(0, bk, sub):
                        c = piece(src, hh, j, off, sub, False, c)
                    new.append(c)
                return tuple(new)

            carry = tuple((jnp.zeros((bq, LANES), F32), jnp.zeros((bq, HEAD_DIM), F32)) for _ in heads)
            carry = all_blocks(block, carry)
            outs = []
            for hh in heads:
                lp, acc = piece(cur, hh, qi, 0, bq, True, carry[hh])
                outs.append(acc / jnp.sum(lp, axis=-1, keepdims=True))
            finish(outs)
        else:
            def update(s, v, carry_h):
                m, l, acc = carry_h
                m_new = jnp.maximum(m, jnp.max(s, axis=-1, keepdims=True))
                alpha = jnp.exp2(m - m_new)
                p = jnp.exp2(s - m_new)
                l = alpha * l + jnp.sum(p, axis=-1, keepdims=True)
                acc = alpha * acc + jnp.dot(p.astype(BF16), v, preferred_element_type=F32)
                return m_new, l, acc

            def scores(src, hh, j):
                load, k_ref, _, ck_ref = src
                bk = ck_ref.shape[4]
                start = pl.multiple_of(j * bk, bk)
                s = lax.dot_general(qs[hh], load(k_ref, hh, start, bk), (((1,), (1,)), ((), ())),
                                    preferred_element_type=F32)
                return s - ck_ref[0, hh, j] * LOG2E, start, bk

            def block(src, j, carry):
                new = []
                for hh in heads:
                    s, start, bk = scores(src, hh, j)
                    new.append(update(s, src[0](src[2], hh, start, bk), carry[hh]))
                return tuple(new)

            carry = tuple((jnp.full((bq, 1), NEG_BIG, F32), jnp.zeros((bq, 1), F32),
                           jnp.zeros((bq, HEAD_DIM), F32)) for _ in heads)
            carry = all_blocks(block, carry)
            row = lax.broadcasted_iota(jnp.int32, (bq, bq), 0)
            col = lax.broadcasted_iota(jnp.int32, (bq, bq), 1)
            outs = []
            for hh in heads:
                s, start, bk = scores(cur, hh, qi)
                s = jnp.where(row >= col, s, NEG_BIG)
                _, l, acc = update(s, load_cur(vc_ref, hh, start, bk), carry[hh])
                outs.append(acc / l)
            finish(outs)
        return 0

    small_logits = jnp.max(smax2) * 2.0 <= BOUNDED_LOGIT_RANGE

    @pl.when(small_logits)
    def _():
        lax.fori_loop(0, nq, lambda qi, _: q_block(qi, False), 0)

    @pl.when(jnp.logical_not(small_logits))
    def _():
        lax.fori_loop(0, nq, lambda qi, _: q_block(qi, True), 0)


def _row_blocks(cum_rows, blk):
    b, h, t = cum_rows.shape
    rows = cum_rows.reshape(b, h, t // blk, 1, blk)
    return rows, rows[:, :, :, 0, blk - 1].reshape(b, h, 1, t // blk)


def _attn(q, k_cur, v_cur, cum, cum_rows, za, gq, gk, bq, past=None):
    b, tq, _ = q.shape
    hps = LANES // HEAD_DIM
    nq = tq // bq
    ckc, cec = _row_blocks(cum_rows, bq)
    qspec = pl.BlockSpec((1, tq, LANES), lambda i, h: (i, 0, h))
    kvspec = pl.BlockSpec((1, hps, tq, HEAD_DIM), lambda i, h: (i, h, 0, 0))
    rowspec = lambda n, w: pl.BlockSpec((1, hps, n, 1, w), lambda i, h: (i, h, 0, 0, 0))
    endspec = lambda n: pl.BlockSpec((1, hps, 1, n), lambda i, h: (i, h, 0, 0))
    gspec = pl.BlockSpec((1, ATTN_WIDTH), lambda i, h: (0, 0))
    in_specs = [qspec, kvspec, kvspec, pl.BlockSpec((1, tq, N_HEADS), lambda i, h: (i, 0, 0)),
                rowspec(nq, bq), endspec(nq), gspec, gspec, qspec]
    args = [q, k_cur, v_cur, cum, ckc, cec, gq, gk, za]
    if past is not None:
        k_past, v_past, cum_past_rows, bkp = past
        tp = k_past.shape[1]
        ckp, cep = _row_blocks(cum_past_rows, bkp)
        pastspec = pl.BlockSpec((1, tp, LANES), lambda i, h: (i, 0, h))
        in_specs += [pastspec, pastspec, rowspec(tp // bkp, bkp), endspec(tp // bkp)]
        args += [k_past, v_past, ckp, cep]
    return pl.pallas_call(
        functools.partial(_attn_kernel, bq=bq, has_past=past is not None),
        grid=(b, N_HEADS // hps),
        in_specs=in_specs,
        out_specs=qspec,
        out_shape=jax.ShapeDtypeStruct((b, tq, ATTN_WIDTH), BF16),
        compiler_params=_cparams(("arbitrary", "arbitrary")),
        name="attn",
    )(*args)


def _out_kernel(x_ref, mod_ref, ys_ref, zs_ref, ya_ref, wglu_ref, bglu_ref, wo_ref, o_ref):
    y = ys_ref[0]
    g = y * (0.5 * (1.0 + jnp.tanh(GELU_C * (y + 0.044715 * (y * y * y)))))
    glu = jnp.dot(g.astype(BF16), wglu_ref[...], preferred_element_type=F32) + bglu_ref[...]
    y1 = g * _sigmoid(glu) * _silu(zs_ref[0].astype(F32))
    mixed = (jnp.dot(y1.astype(BF16), wo_ref[0:SSM_WIDTH, :], preferred_element_type=F32)
             + jnp.dot(ya_ref[0], wo_ref[SSM_WIDTH:, :], preferred_element_type=F32))
    o_ref[0] = x_ref[0] + mod_ref[0, 2:3, :] * mixed


def _out(x, mod3, y_ssm, zs, y_att, w_glu, b_glu, w_out, tm):
    b, t, d = x.shape
    const = lambda shape: pl.BlockSpec(shape, lambda i, j: (0,) * len(shape))
    tokspec = lambda w: pl.BlockSpec((1, tm, w), lambda i, j: (i, j, 0))
    return pl.pallas_call(
        _out_kernel,
        grid=(b, t // tm),
        in_specs=[tokspec(d), pl.BlockSpec((1, 3, d), lambda i, j: (i, 0, 0)),
                  tokspec(SSM_WIDTH), tokspec(SSM_WIDTH), tokspec(ATTN_WIDTH),
                  const(w_glu.shape), const((1, SSM_WIDTH)), const(w_out.shape)],
        out_specs=tokspec(d),
        out_shape=jax.ShapeDtypeStruct((b, t, d), F32),
        compiler_params=_cparams(("arbitrary", "arbitrary")),
        name="out",
    )(x, mod3, y_ssm, zs, y_att, w_glu, b_glu, w_out)


def _tile(t, pref):
    return pref if t % pref == 0 else t


def _layer(x, mod3, h0, past, p):
    b, t, d = x.shape
    tm = _tile(t, 512)
    if past is None:
        cum_init = jnp.zeros((b, 1, LANES), F32)
        attn_past = None
    else:
        ck_past, cv_past, clogf = past
        tp = ck_past.shape[1]
        cum_past_rows = _cumsum_rows(jnp.transpose(clogf.astype(F32), (0, 2, 1)), _tile(tp, 512))
        cum_init = jnp.pad(cum_past_rows[:, :, tp - 1], ((0, 0), (0, LANES - N_HEADS))).reshape(b, 1, LANES)
        attn_past = (ck_past.reshape(b, tp, ATTN_WIDTH), cv_past.reshape(b, tp, ATTN_WIDTH),
                     cum_past_rows, _tile(tp, 512))

    u, zs, q, k_out, v_out, k_att, v_att, za, logf, cum, cum_rows = _inproj(
        x, mod3, p['norm_g'], p['w_main'], p['w_f'], p['b_f'], p['gq'], p['gk'], p['e_mat'], cum_init, tm)
    y_ssm, h_new = _s5(u, h0, p['are_b'], p['aim_b'], p['bblk'], p['cblk'], p['d_row'], _tile(t, 128))
    y_att = _attn(q, k_att, v_att, cum, cum_rows, za, p['gq'], p['gk'], _tile(t, 512), attn_past)
    y = _out(x, mod3, y_ssm, zs, y_att, p['w_glu'], p['b_glu'], p['w_out'], tm)
    return y, k_out, v_out, logf, h_new


def kernel(x_prompt, x_sample, cache_k, cache_v, cache_logf, state_ssm_re, state_ssm_im, c_prompt, c_sample,
           w_ada, b_ada, norm_g, w_in, b_f, q_norm_g, k_norm_g, ssm_log_dt, ssm_a_re, ssm_a_im,
           ssm_b_re, ssm_b_im, ssm_c_re, ssm_c_im, ssm_d, w_glu, b_glu, w_out):
    depth = w_ada.shape[0]
    d = x_prompt.shape[-1]
    bp = x_prompt.shape[0]
    bs = x_sample.shape[0]
    xp, xs = x_prompt, x_sample
    outs_p, outs_s = [], []
    for l in range(depth):
        mod = _mod(jnp.concatenate([c_prompt, c_sample], axis=0), w_ada[l], b_ada[l])
        mod3 = mod.reshape(bp + bs, 3, d)
        abar_re, abar_im, bbar_re, bbar_im = _zoh(ssm_log_dt[l], ssm_a_re[l], ssm_a_im[l],
                                                  ssm_b_re[l], ssm_b_im[l])
        a_lanes_re = _state_layout(abar_re, abar_re)
        a_lanes_im = _state_layout(abar_im, abar_im)
        split = 2 * SSM_WIDTH + 4 * ATTN_WIDTH
        hd = jnp.arange(ATTN_WIDTH) // HEAD_DIM
        p = dict(
            norm_g=norm_g[l].reshape(1, d),
            w_main=w_in[l][:, :split].astype(BF16),
            w_f=jnp.pad(w_in[l][:, split:], ((0, 0), (0, LANES - N_HEADS))).astype(BF16),
            b_f=jnp.pad(b_f[l], (0, LANES - N_HEADS)).reshape(1, LANES),
            gq=jnp.tile(q_norm_g[l], N_HEADS).reshape(1, ATTN_WIDTH),
            gk=jnp.tile(k_norm_g[l], N_HEADS).reshape(1, ATTN_WIDTH),
            e_mat=(hd[:, None] == hd[None, :]).astype(BF16) * (1.0 / HEAD_DIM),
            are_b=jnp.broadcast_to(a_lanes_re[None], (SUBLANES, STATE_LANES)),
            aim_b=jnp.broadcast_to(a_lanes_im[None], (SUBLANES, STATE_LANES)),
            bblk=_block_diag_b(bbar_re, bbar_im),
            cblk=_block_diag_c(ssm_c_re[l], ssm_c_im[l]),
            d_row=ssm_d[l].reshape(1, SSM_WIDTH),
            w_glu=w_glu[l].astype(BF16), b_glu=b_glu[l].reshape(1, SSM_WIDTH),
            w_out=w_out[l].astype(BF16),
        )
        h0_p = jnp.zeros((bp, STATE_LANES), F32)
        xp, k1, v1, f1, h1 = _layer(xp, mod3[:bp], h0_p, None, p)
        h0_s = _state_layout(state_ssm_re[l], state_ssm_im[l])
        xs, k2, v2, f2, h2 = _layer(xs, mod3[bp:], h0_s, (cache_k[l], cache_v[l], cache_logf[l]), p)
        outs_p.append((k1, v1, f1) + _state_unlayout(h1))
        outs_s.append((k2, v2, f2) + _state_unlayout(h2))

    def stack(items, i, shape_tail):
        arr = jnp.stack([it[i] for it in items])
        return arr.reshape(arr.shape[:3] + shape_tail) if shape_tail else arr

    hd_tail = (N_HEADS, HEAD_DIM)
    return (xp, xs,
            stack(outs_p, 0, hd_tail), stack(outs_p, 1, hd_tail), stack(outs_p, 2, ()),
            stack(outs_p, 3, ()), stack(outs_p, 4, ()),
            stack(outs_s, 0, hd_tail), stack(outs_s, 1, hd_tail), stack(outs_s, 2, ()),
            stack(outs_s, 3, ()), stack(outs_s, 4, ()))
```

```python
import functools

import numpy as np
import jax
import jax.numpy as jnp
from jax import lax
from jax.experimental import pallas as pl
from jax.experimental.pallas import tpu as pltpu

F32 = jnp.float32
BF16 = jnp.bfloat16

N_HEADS = 8
HEAD_DIM = 64
ATTN_WIDTH = N_HEADS * HEAD_DIM
SSM_GROUPS = 32
SSM_GROUP = 16
SSM_STATE = 64
SSM_WIDTH = SSM_GROUPS * SSM_GROUP
STATE_LANES = 2 * SSM_GROUPS * SSM_STATE
HALF_LANES = STATE_LANES // 2
HALF_CH = SSM_WIDTH // 2
ATTN_SCALE = HEAD_DIM ** -0.5
NORM_EPS = 1e-6
LANES = 128
SUBLANES = 8
NEG_BIG = -1e30
LOG2E = float(np.log2(np.e))
GELU_C = float(np.float32(np.sqrt(2.0 / np.pi)))
VMEM_LIMIT = 56 * 1024 * 1024
SKIP_LOGIT_GAP = 105.0
BF16_NORM_SLACK = 1.02
S5_PHASES = 2
ATTN_SUB = 512
BOUNDED_LOGIT_RANGE = 96.0


def _cparams(sem):
    return pltpu.CompilerParams(dimension_semantics=sem, vmem_limit_bytes=VMEM_LIMIT)


def _sigmoid(x):
    return 1.0 / (1.0 + jnp.exp(-x))


def _silu(x):
    return x * _sigmoid(x)


def _split3(x):
    hi = x.astype(BF16)
    r1 = x - hi.astype(F32)
    mid = r1.astype(BF16)
    lo = (r1 - mid.astype(F32)).astype(BF16)
    return hi, mid, lo


def _tri(n):
    return jnp.tril(jnp.ones((n, n), F32)).astype(BF16)


def _mod_kernel(c_ref, w_ref, b_ref, o_ref):
    c = c_ref[...]
    o_ref[...] = jnp.dot(_silu(c), w_ref[...], preferred_element_type=F32) + b_ref[...]


def _mod(c, w_ada, b_ada):
    n, d = c.shape
    n_out = w_ada.shape[1]
    blk = 1024
    return pl.pallas_call(
        _mod_kernel,
        grid=(n_out // blk,),
        in_specs=[pl.BlockSpec((n, d), lambda j: (0, 0)),
                  pl.BlockSpec((d, blk), lambda j: (0, j)),
                  pl.BlockSpec((1, blk), lambda j: (0, j))],
        out_specs=pl.BlockSpec((n, blk), lambda j: (0, j)),
        out_shape=jax.ShapeDtypeStruct((n, n_out), F32),
        compiler_params=_cparams(("arbitrary",)),
        name="mod",
    )(c, w_ada, b_ada.reshape(1, n_out))


def _zoh_kernel(ldt_ref, are_ref, aim_ref, bre_ref, bim_ref,
                abre_ref, abim_ref, bbre_ref, bbim_ref):
    dt = jnp.exp(ldt_ref[...])
    a_re = are_ref[...]
    a_im = aim_ref[...]
    mag = jnp.exp(a_re * dt)
    ang = a_im * dt
    abar_re = mag * jnp.cos(ang)
    abar_im = mag * jnp.sin(ang)
    den = a_re * a_re + a_im * a_im
    n_re = abar_re - 1.0
    n_im = abar_im
    q_re = (n_re * a_re + n_im * a_im) / den
    q_im = (n_im * a_re - n_re * a_im) / den
    b_re = bre_ref[...]
    b_im = bim_ref[...]
    abre_ref[...] = abar_re
    abim_ref[...] = abar_im
    bbre_ref[...] = q_re * b_re - q_im * b_im
    bbim_ref[...] = q_re * b_im + q_im * b_re


def _zoh(log_dt, a_re, a_im, b_re, b_im):
    g, n, p = b_re.shape
    rows = g * n
    ldt = jnp.broadcast_to(log_dt[:, None], (g, n)).reshape(rows, 1)
    outs = pl.pallas_call(
        _zoh_kernel,
        out_shape=[jax.ShapeDtypeStruct((rows, 1), F32), jax.ShapeDtypeStruct((rows, 1), F32),
                   jax.ShapeDtypeStruct((rows, p), F32), jax.ShapeDtypeStruct((rows, p), F32)],
        name="zoh",
    )(ldt, a_re.reshape(rows, 1), a_im.reshape(rows, 1), b_re.reshape(rows, p), b_im.reshape(rows, p))
    abar_re, abar_im, bbar_re, bbar_im = outs
    return (abar_re.reshape(g, n), abar_im.reshape(g, n),
            bbar_re.reshape(g, n, p), bbar_im.reshape(g, n, p))


def _state_layout(x_re, x_im):
    lead = x_re.shape[:-2]
    hg = SSM_GROUPS // 2
    r = x_re.reshape(lead + (2, 1, hg * SSM_STATE))
    i = x_im.reshape(lead + (2, 1, hg * SSM_STATE))
    return jnp.concatenate([r, i], axis=-2).reshape(lead + (STATE_LANES,))


def _state_unlayout(h):
    lead = h.shape[:-1]
    hg = SSM_GROUPS // 2
    x = h.reshape(lead + (2, 2, hg, SSM_STATE))
    re = x[..., :, 0, :, :].reshape(lead + (SSM_GROUPS, SSM_STATE))
    im = x[..., :, 1, :, :].reshape(lead + (SSM_GROUPS, SSM_STATE))
    return re, im


def _block_diag_b(bbar_re, bbar_im):
    hg = SSM_GROUPS // 2
    eye = jnp.eye(hg, dtype=F32)
    out = []
    for hf in range(2):
        parts = []
        for bb in (bbar_re, bbar_im):
            b = bb[hf * hg:(hf + 1) * hg]
            m = jnp.einsum('gnp,gk->gpkn', b, eye)
            parts.append(m.reshape(hg * SSM_GROUP, hg * SSM_STATE))
        out.append(jnp.concatenate(parts, axis=1))
    return jnp.stack(out).astype(BF16)


def _block_diag_c(c_re, c_im):
    hg = SSM_GROUPS // 2
    eye = jnp.eye(hg, dtype=F32)
    out = []
    for hf in range(2):
        parts = []
        for cc in (c_re, -c_im):
            c = cc[hf * hg:(hf + 1) * hg]
            m = jnp.einsum('gpn,gk->gnkp', c, eye)
            parts.append(m.reshape(hg * SSM_STATE, hg * SSM_GROUP))
        out.append(jnp.concatenate(parts, axis=0))
    return jnp.stack(out).astype(BF16)


def _inproj_kernel(x_ref, mod_ref, g_ref, wm_ref, wf_ref, bf_ref, gq_ref, gk_ref, e_ref, tri_ref, cinit_ref,
                   u_ref, zs_ref, q_ref, kout_ref, vout_ref, katt_ref, vatt_ref, za_ref, cum_ref, logf_ref,
                   *rest, rows):
    if rows:
        cum_rows_ref, carry_ref = rest
    else:
        carry_ref, = rest

    @pl.when(pl.program_id(1) == 0)
    def _():
        carry_ref[...] = cinit_ref[0]

    x = x_ref[0]
    ms = jnp.mean(x * x, axis=-1, keepdims=True)
    xn = x * lax.rsqrt(ms + NORM_EPS) * g_ref[...]
    shift = mod_ref[0, 0:1, :]
    scale = mod_ref[0, 1:2, :]
    hb = (xn * (1.0 + scale) + shift).astype(BF16)

    def sec(i):
        return jnp.dot(hb, wm_ref[:, i * 512:(i + 1) * 512], preferred_element_type=F32)

    def head_rmsnorm(y, g):
        msq = jnp.dot((y * y).astype(BF16), e_ref[...], preferred_element_type=F32)
        return y * lax.rsqrt(msq + NORM_EPS) * g

    u_ref[0] = sec(0)
    zs_ref[0] = sec(1).astype(BF16)
    q = head_rmsnorm(sec(2), gq_ref[...])
    q_ref[0] = (q * (ATTN_SCALE * LOG2E)).astype(BF16)
    k = head_rmsnorm(sec(3), gk_ref[...])
    kout_ref[0] = k
    v = sec(4)
    vout_ref[0] = v
    kb = k.astype(BF16)
    vb = v.astype(BF16)
    for h in range(N_HEADS):
        katt_ref[0, h] = kb[:, h * HEAD_DIM:(h + 1) * HEAD_DIM]
        vatt_ref[0, h] = vb[:, h * HEAD_DIM:(h + 1) * HEAD_DIM]
    za_ref[0] = sec(5).astype(BF16)
    f = jnp.dot(hb, wf_ref[...], preferred_element_type=F32) + bf_ref[...]
    logf = jnp.minimum(f, 0.0) - jnp.log1p(jnp.exp(-jnp.abs(f)))
    if rows:
        logf_ref[0] = logf.T[:N_HEADS, :]
    else:
        logf_ref[0] = logf[:, :N_HEADS]
    lane = lax.broadcasted_iota(jnp.int32, logf.shape, 1)
    hi, mid, lo = (part.astype(F32) for part in _split3(jnp.where(lane < N_HEADS, logf, 0.0)))
    packed = hi + pltpu.roll(mid, N_HEADS, 1) + pltpu.roll(lo, 2 * N_HEADS, 1)
    r = jnp.dot(tri_ref[...], packed.astype(BF16), preferred_element_type=F32)
    c = carry_ref[...] + r + pltpu.roll(r, LANES - N_HEADS, 1) + pltpu.roll(r, LANES - 2 * N_HEADS, 1)
    cum_ref[0] = c[:, :N_HEADS]
    if rows:
        cum_rows_ref[0] = c.T[:N_HEADS, :]
    tm = c.shape[0]
    carry_ref[...] = c[tm - 1:tm, :]


def _inproj(x, mod3, norm_g, w_main, w_f, b_f, gq, gk, e_mat, cum_init, tm):
    b, t, d = x.shape
    nt = t // tm
    tok = lambda w, dt: jax.ShapeDtypeStruct((b, t, w), dt)
    heads = jax.ShapeDtypeStruct((b, N_HEADS, t, HEAD_DIM), BF16)
    const = lambda shape: pl.BlockSpec(shape, lambda i, j: (0,) * len(shape))
    tokspec = lambda w: pl.BlockSpec((1, tm, w), lambda i, j: (i, j, 0))
    headspec = pl.BlockSpec((1, N_HEADS, tm, HEAD_DIM), lambda i, j: (i, 0, j, 0))
    rows = tm % LANES == 0
    rowspec = pl.BlockSpec((1, N_HEADS, tm), lambda i, j: (i, 0, j))
    rowshape = jax.ShapeDtypeStruct((b, N_HEADS, t), F32)
    out_specs = [tokspec(SSM_WIDTH), tokspec(SSM_WIDTH), tokspec(ATTN_WIDTH), tokspec(ATTN_WIDTH),
                 tokspec(ATTN_WIDTH), headspec, headspec, tokspec(ATTN_WIDTH), tokspec(N_HEADS)]
    out_shape = [tok(SSM_WIDTH, F32), tok(SSM_WIDTH, BF16), tok(ATTN_WIDTH, BF16), tok(ATTN_WIDTH, F32),
                 tok(ATTN_WIDTH, F32), heads, heads, tok(ATTN_WIDTH, BF16), tok(N_HEADS, F32)]
    out_specs += [rowspec, rowspec] if rows else [tokspec(N_HEADS)]
    out_shape += [rowshape, rowshape] if rows else [tok(N_HEADS, F32)]
    outs = pl.pallas_call(
        functools.partial(_inproj_kernel, rows=rows),
        grid=(b, nt),
        in_specs=[tokspec(d),
                  pl.BlockSpec((1, 3, d), lambda i, j: (i, 0, 0)),
                  const((1, d)), const(w_main.shape), const(w_f.shape), const((1, LANES)),
                  const((1, ATTN_WIDTH)), const((1, ATTN_WIDTH)), const(e_mat.shape), const((tm, tm)),
                  pl.BlockSpec((1, 1, LANES), lambda i, j: (i, 0, 0))],
        out_specs=out_specs,
        out_shape=out_shape,
        scratch_shapes=[pltpu.VMEM((1, LANES), F32)],
        compiler_params=_cparams(("arbitrary", "arbitrary")),
        name="inproj",
    )(x, mod3, norm_g, w_main, w_f, b_f, gq, gk, e_mat, _tri(tm), cum_init)
    main, cum = outs[:8], outs[8]
    if rows:
        logf_rows, cum_rows = outs[9:]
        logf = jnp.transpose(logf_rows, (0, 2, 1))
    else:
        logf = outs[9]
        cum_rows = jnp.transpose(cum, (0, 2, 1))
    return (*main, logf, cum, cum_rows)


def _cumsum_rows_kernel(x_ref, triu_ref, o_ref):
    tc = triu_ref.shape[0]
    total = jnp.zeros((x_ref.shape[1], 1), F32)
    for c0 in range(0, x_ref.shape[2], tc):
        c = total
        for part in _split3(x_ref[0, :, c0:c0 + tc]):
            c = c + jnp.dot(part, triu_ref[...], preferred_element_type=F32)
        o_ref[0, :, c0:c0 + tc] = c
        total = c[:, tc - 1:tc]


def _cumsum_rows(x, tc):
    b, h, t = x.shape
    return pl.pallas_call(
        _cumsum_rows_kernel,
        grid=(b,),
        in_specs=[pl.BlockSpec((1, h, t), lambda i: (i, 0, 0)),
                  pl.BlockSpec((tc, tc), lambda i: (0, 0))],
        out_specs=pl.BlockSpec((1, h, t), lambda i: (i, 0, 0)),
        out_shape=jax.ShapeDtypeStruct((b, h, t), F32),
        compiler_params=_cparams(("arbitrary",)),
        name="cumsum",
    )(x, jnp.triu(jnp.ones((tc, tc), F32)).astype(BF16))


def _s5_kernel(u_ref, h0_ref, are_ref, aim_ref, bblk_ref, cblk_ref, d_ref,
               y_ref, hout_ref,
               ut_ref, utm_ref, bu_ref, ytm_ref, *, tt, pitch):
    nb = u_ref.shape[0]
    nslab = SSM_WIDTH // LANES

    @pl.when(pl.program_id(0) == 0)
    def _():
        hout_ref[...] = h0_ref[...]

    for b in range(nb):
        for k in range(nslab):
            ut_ref[k, b * pitch:b * pitch + tt, :] = u_ref[b, :, k * LANES:(k + 1) * LANES]

    nph = S5_PHASES if tt % (S5_PHASES * SUBLANES) == 0 else 1
    tp = tt // nph
    chunk = 4 * LANES
    quarter = HALF_LANES // 2
    chunks = [(hf * HALF_LANES + c * chunk, hf * HALF_LANES + c * chunk + quarter)
              for hf in range(2) for c in range(quarter // chunk)]

    def b_proj(ph):
        for t in range(ph * tp, (ph + 1) * tp):
            for k in range(nslab):
                utm_ref[t * nb:(t + 1) * nb, k * LANES:(k + 1) * LANES] = ut_ref[k, pl.ds(t, nb, stride=pitch), :]
        rows = slice(ph * tp * nb, (ph + 1) * tp * nb)
        for hf in range(2):
            uh = utm_ref[rows, hf * HALF_CH:(hf + 1) * HALF_CH].astype(BF16)
            bu_ref[rows, hf * HALF_LANES:(hf + 1) * HALF_LANES] = jnp.dot(
                uh, bblk_ref[hf], preferred_element_type=F32)

    def scan(ph):
        state = [(hout_ref[:, lo_re:lo_re + chunk], hout_ref[:, lo_im:lo_im + chunk]) for lo_re, lo_im in chunks]
        for t in range(ph * tp, (ph + 1) * tp):
            r = slice(t * nb, (t + 1) * nb)
            for ci, (lo_re, lo_im) in enumerate(chunks):
                xr, xi = state[ci]
                ar = are_ref[:, lo_re:lo_re + chunk]
                ai = aim_ref[:, lo_re:lo_re + chunk]
                nr = ar * xr - ai * xi + bu_ref[r, lo_re:lo_re + chunk]
                ni = ar * xi + ai * xr + bu_ref[r, lo_im:lo_im + chunk]
                bu_ref[r, lo_re:lo_re + chunk] = nr
                bu_ref[r, lo_im:lo_im + chunk] = ni
                state[ci] = (nr, ni)
        for (lo_re, lo_im), (xr, xi) in zip(chunks, state):
            hout_ref[:, lo_re:lo_re + chunk] = xr
            hout_ref[:, lo_im:lo_im + chunk] = xi

    def c_proj(ph):
        rows = slice(ph * tp * nb, (ph + 1) * tp * nb)
        for hf in range(2):
            xh = bu_ref[rows, hf * HALF_LANES:(hf + 1) * HALF_LANES].astype(BF16)
            yh = jnp.dot(xh, cblk_ref[hf], preferred_element_type=F32)
            yh = yh + d_ref[:, hf * HALF_CH:(hf + 1) * HALF_CH] * utm_ref[rows, hf * HALF_CH:(hf + 1) * HALF_CH]
            for kk in range(HALF_CH // LANES):
                ytm_ref[hf * (HALF_CH // LANES) + kk, rows, :] = yh[:, kk * LANES:(kk + 1) * LANES]
        for b in range(nb):
            for k in range(nslab):
                y_ref[b, ph * tp:(ph + 1) * tp, k * LANES:(k + 1) * LANES] = (
                    ytm_ref[k, pl.ds(ph * tp * nb + b, tp, stride=nb), :]).astype(y_ref.dtype)

    b_proj(0)
    for ph in range(nph):
        if ph + 1 < nph:
            b_proj(ph + 1)
        scan(ph)
        c_proj(ph)


def _s5(u, h0, are_b, aim_b, bblk, cblk, d_row, tt):
    b, t, w = u.shape
    assert b == SUBLANES and t % tt == 0 and tt % SUBLANES == 0
    pitch = tt + SUBLANES
    rows = tt * b
    const = lambda shape: pl.BlockSpec(shape, lambda j: (0,) * len(shape))
    return pl.pallas_call(
        functools.partial(_s5_kernel, tt=tt, pitch=pitch),
        grid=(t // tt,),
        in_specs=[pl.BlockSpec((b, tt, w), lambda j: (0, j, 0)),
                  const((b, STATE_LANES)), const((b, STATE_LANES)), const((b, STATE_LANES)),
                  const(bblk.shape), const(cblk.shape), const((1, w))],
        out_specs=[pl.BlockSpec((b, tt, w), lambda j: (0, j, 0)), const((b, STATE_LANES))],
        out_shape=[jax.ShapeDtypeStruct((b, t, w), BF16), jax.ShapeDtypeStruct((b, STATE_LANES), F32)],
        scratch_shapes=[pltpu.VMEM((w // LANES, b * pitch, LANES), F32),
                        pltpu.VMEM((rows, w), F32),
                        pltpu.VMEM((rows, STATE_LANES), F32),
                        pltpu.VMEM((w // LANES, rows, LANES), F32)],
        compiler_params=_cparams(("arbitrary",)),
        name="s5",
    )(u, h0, are_b, aim_b, bblk, cblk, d_row)


def _attn_kernel(*refs, bq, has_past):
    if has_past:
        (q_ref, kc_ref, vc_ref, cq_ref, ckc_ref, cec_ref, gq_ref, gk_ref, za_ref,
         kp_ref, vp_ref, ckp_ref, cep_ref, o_ref) = refs
    else:
        q_ref, kc_ref, vc_ref, cq_ref, ckc_ref, cec_ref, gq_ref, gk_ref, za_ref, o_ref = refs
    hp = pl.program_id(1)
    heads_per_step = q_ref.shape[2] // HEAD_DIM
    heads = range(heads_per_step)
    nq = q_ref.shape[1] // bq

    smax = (jnp.max(jnp.abs(gq_ref[...]), axis=-1, keepdims=True)
            * jnp.max(jnp.abs(gk_ref[...]), axis=-1, keepdims=True)
            * (HEAD_DIM * ATTN_SCALE * BF16_NORM_SLACK * BF16_NORM_SLACK))
    skip_below = -(SKIP_LOGIT_GAP + 2.0 * smax)
    smax2 = smax * LOG2E

    def load_cur(ref, hh, start, n):
        return ref[0, hh, pl.ds(start, n), :]

    def load_past(ref, hh, start, n):
        return ref[0, pl.ds(start, n), hh * HEAD_DIM:(hh + 1) * HEAD_DIM].astype(BF16)

    cur = (load_cur, kc_ref, vc_ref, ckc_ref)
    past = (load_past, kp_ref, vp_ref, ckp_ref) if has_past else None

    def q_block(qi, online):
        row0 = pl.multiple_of(qi * bq, bq)
        qrows = pl.ds(row0, bq)
        cq_all = cq_ref[0, qrows, :]
        head_lane = lax.broadcasted_iota(jnp.int32, cq_all.shape, 1)
        qs = [q_ref[0, qrows, hh * HEAD_DIM:(hh + 1) * HEAD_DIM] for hh in heads]

        def cq_of(hh, rows):
            h = hp * heads_per_step + hh
            return jnp.sum(jnp.where(head_lane[rows] == h, cq_all[rows], 0.0), axis=-1, keepdims=True)

        def first_live(ce_ref, n_visible):
            dead = None
            for hh in heads:
                cq_first = cq_of(hh, slice(0, 1))
                ce = ce_ref[0, hh]
                idx = lax.broadcasted_iota(jnp.int32, ce.shape, 1)
                d = jnp.logical_and(cq_first - ce < skip_below, idx < n_visible)
                dead = d if dead is None else jnp.logical_and(dead, d)
            return jnp.sum(dead.astype(jnp.int32))

        def finish(outs):
            o = jnp.concatenate(outs, axis=-1)
            o_ref[0, qrows, :] = (o * _silu(za_ref[0, qrows, :].astype(F32))).astype(BF16)

        def sweep(block, src, lo, hi, carry):
            odd = lo + jnp.bitwise_and(hi - lo, 1)
            carry = lax.fori_loop(lo, odd, lambda j, c: block(src, j, c), carry)

            def pair(i, c):
                j = odd + 2 * i
                return block(src, j + 1, block(src, j, c))
            return lax.fori_loop(0, lax.shift_right_logical(hi - odd, 1), pair, carry)

        def all_blocks(block, carry):
            if has_past:
                npb = ckp_ref.shape[2]
                carry = sweep(block, past, first_live(cep_ref, npb), npb, carry)
            return sweep(block, cur, first_live(cec_ref, qi), qi, carry)

        if not online:
            refs_row = [smax2 - cq_of(hh, slice(None)) * LOG2E for hh in heads]

            def piece(src, hh, j, off, sub, causal, carry_h):
                load, k_ref, v_ref, ck_ref = src
                lp, acc = carry_h
                bk = ck_ref.shape[4]
                start = pl.multiple_of(j * bk + off, sub)
                s = lax.dot_general(qs[hh], load(k_ref, hh, start, sub), (((1,), (1,)), ((), ())),
                                    preferred_element_type=F32)
                t = s - ck_ref[0, hh, j][:, off:off + sub] * LOG2E - refs_row[hh]
                if causal:
                    r = lax.broadcasted_iota(jnp.int32, t.shape, 0)
                    c = lax.broadcasted_iota(jnp.int32, t.shape, 1) + off
                    t = jnp.where(r >= c, t, NEG_BIG)
                p = jnp.exp2(t)
                if sub < LANES:
                    psum = jnp.concatenate([p, jnp.zeros((p.shape[0], LANES - sub), F32)], axis=-1)
                else:
                    psum = p[:, 0:LANES]
                    for c0 in range(LANES, sub, LANES):
                        psum = psum + p[:, c0:c0 + LANES]
                pv = jnp.dot(p.astype(BF16), load(v_ref, hh, start, sub), preferred_element_type=F32)
                return lp + psum, acc + pv

            def block(src, j, carry):
                bk = src[3].shape[4]
                sub = min(bk, ATTN_SUB)
                new = []
                for hh in heads:
                    c = carry[hh]
                    for off in range(0, bk, sub):
                        c = piece(src, hh, j, off, sub, False, c)
                    new.append(c)
                return tuple(new)

            carry = tuple((jnp.zeros((bq, LANES), F32), jnp.zeros((bq, HEAD_DIM), F32)) for _ in heads)
            carry = all_blocks(block, carry)
            outs = []
            for hh in heads:
                lp, acc = piece(cur, hh, qi, 0, bq, True, carry[hh])
                outs.append(acc / jnp.sum(lp, axis=-1, keepdims=True))
            finish(outs)
        else:
            def update(s, v, carry_h):
                m, l, acc = carry_h
                m_new = jnp.maximum(m, jnp.max(s, axis=-1, keepdims=True))
                alpha = jnp.exp2(m - m_new)
                p = jnp.exp2(s - m_new)
                l = alpha * l + jnp.sum(p, axis=-1, keepdims=True)
                acc = alpha * acc + jnp.dot(p.astype(BF16), v, preferred_element_type=F32)
                return m_new, l, acc

            def scores(src, hh, j):
                load, k_ref, _, ck_ref = src
                bk = ck_ref.shape[4]
                start = pl.multiple_of(j * bk, bk)
                s = lax.dot_general(qs[hh], load(k_ref, hh, start, bk), (((1,), (1,)), ((), ())),
                                    preferred_element_type=F32)
                return s - ck_ref[0, hh, j] * LOG2E, start, bk

            def block(src, j, carry):
                new = []
                for hh in heads:
                    s, start, bk = scores(src, hh, j)
                    new.append(update(s, src[0](src[2], hh, start, bk), carry[hh]))
                return tuple(new)

            carry = tuple((jnp.full((bq, 1), NEG_BIG, F32), jnp.zeros((bq, 1), F32),
                           jnp.zeros((bq, HEAD_DIM), F32)) for _ in heads)
            carry = all_blocks(block, carry)
            row = lax.broadcasted_iota(jnp.int32, (bq, bq), 0)
            col = lax.broadcasted_iota(jnp.int32, (bq, bq), 1)
            outs = []
            for hh in heads:
                s, start, bk = scores(cur, hh, qi)
                s = jnp.where(row >= col, s, NEG_BIG)
                _, l, acc = update(s, load_cur(vc_ref, hh, start, bk), carry[hh])
                outs.append(acc / l)
            finish(outs)
        return 0

    small_logits = jnp.max(smax2) * 2.0 <= BOUNDED_LOGIT_RANGE

    @pl.when(small_logits)
    def _():
        lax.fori_loop(0, nq, lambda qi, _: q_block(qi, False), 0)

    @pl.when(jnp.logical_not(small_logits))
    def _():
        lax.fori_loop(0, nq, lambda qi, _: q_block(qi, True), 0)


def _row_blocks(cum_rows, blk):
    b, h, t = cum_rows.shape
    rows = cum_rows.reshape(b, h, t // blk, 1, blk)
    return rows, rows[:, :, :, 0, blk - 1].reshape(b, h, 1, t // blk)


def _attn(q, k_cur, v_cur, cum, cum_rows, za, gq, gk, bq, past=None):
    b, tq, _ = q.shape
    hps = LANES // HEAD_DIM
    nq = tq // bq
    ckc, cec = _row_blocks(cum_rows, bq)
    qspec = pl.BlockSpec((1, tq, LANES), lambda i, h: (i, 0, h))
    kvspec = pl.BlockSpec((1, hps, tq, HEAD_DIM), lambda i, h: (i, h, 0, 0))
    rowspec = lambda n, w: pl.BlockSpec((1, hps, n, 1, w), lambda i, h: (i, h, 0, 0, 0))
    endspec = lambda n: pl.BlockSpec((1, hps, 1, n), lambda i, h: (i, h, 0, 0))
    gspec = pl.BlockSpec((1, ATTN_WIDTH), lambda i, h: (0, 0))
    in_specs = [qspec, kvspec, kvspec, pl.BlockSpec((1, tq, N_HEADS), lambda i, h: (i, 0, 0)),
                rowspec(nq, bq), endspec(nq), gspec, gspec, qspec]
    args = [q, k_cur, v_cur, cum, ckc, cec, gq, gk, za]
    if past is not None:
        k_past, v_past, cum_past_rows, bkp = past
        tp = k_past.shape[1]
        ckp, cep = _row_blocks(cum_past_rows, bkp)
        pastspec = pl.BlockSpec((1, tp, LANES), lambda i, h: (i, 0, h))
        in_specs += [pastspec, pastspec, rowspec(tp // bkp, bkp), endspec(tp // bkp)]
        args += [k_past, v_past, ckp, cep]
    return pl.pallas_call(
        functools.partial(_attn_kernel, bq=bq, has_past=past is not None),
        grid=(b, N_HEADS // hps),
        in_specs=in_specs,
        out_specs=qspec,
        out_shape=jax.ShapeDtypeStruct((b, tq, ATTN_WIDTH), BF16),
        compiler_params=_cparams(("arbitrary", "arbitrary")),
        name="attn",
    )(*args)


def _out_kernel(x_ref, mod_ref, ys_ref, zs_ref, ya_ref, wglu_ref, bglu_ref, wo_ref, o_ref):
    y = ys_ref[0].astype(F32)
    g = y * (0.5 * (1.0 + jnp.tanh(GELU_C * (y + 0.044715 * (y * y * y)))))
    glu = jnp.dot(g.astype(BF16), wglu_ref[...], preferred_element_type=F32) + bglu_ref[...]
    y1 = g * _sigmoid(glu) * _silu(zs_ref[0].astype(F32))
    mixed = (jnp.dot(y1.astype(BF16), wo_ref[0:SSM_WIDTH, :], preferred_element_type=F32)
             + jnp.dot(ya_ref[0], wo_ref[SSM_WIDTH:, :], preferred_element_type=F32))
    o_ref[0] = x_ref[0] + mod_ref[0, 2:3, :] * mixed


def _out(x, mod3, y_ssm, zs, y_att, w_glu, b_glu, w_out, tm):
    b, t, d = x.shape
    const = lambda shape: pl.BlockSpec(shape, lambda i, j: (0,) * len(shape))
    tokspec = lambda w: pl.BlockSpec((1, tm, w), lambda i, j: (i, j, 0))
    return pl.pallas_call(
        _out_kernel,
        grid=(b, t // tm),
        in_specs=[tokspec(d), pl.BlockSpec((1, 3, d), lambda i, j: (i, 0, 0)),
                  tokspec(SSM_WIDTH), tokspec(SSM_WIDTH), tokspec(ATTN_WIDTH),
                  const(w_glu.shape), const((1, SSM_WIDTH)), const(w_out.shape)],
        out_specs=tokspec(d),
        out_shape=jax.ShapeDtypeStruct((b, t, d), F32),
        compiler_params=_cparams(("arbitrary", "arbitrary")),
        name="out",
    )(x, mod3, y_ssm, zs, y_att, w_glu, b_glu, w_out)


def _tile(t, pref):
    return pref if t % pref == 0 else t


def _layer(x, mod3, h0, past, p):
    b, t, d = x.shape
    tm = _tile(t, 512)
    if past is None:
        cum_init = jnp.zeros((b, 1, LANES), F32)
        attn_past = None
    else:
        ck_past, cv_past, clogf = past
        tp = ck_past.shape[1]
        cum_past_rows = _cumsum_rows(jnp.transpose(clogf.astype(F32), (0, 2, 1)), _tile(tp, 512))
        cum_init = jnp.pad(cum_past_rows[:, :, tp - 1], ((0, 0), (0, LANES - N_HEADS))).reshape(b, 1, LANES)
        attn_past = (ck_past.reshape(b, tp, ATTN_WIDTH), cv_past.reshape(b, tp, ATTN_WIDTH),
                     cum_past_rows, _tile(tp, 512))

    u, zs, q, k_out, v_out, k_att, v_att, za, logf, cum, cum_rows = _inproj(
        x, mod3, p['norm_g'], p['w_main'], p['w_f'], p['b_f'], p['gq'], p['gk'], p['e_mat'], cum_init, tm)
    y_ssm, h_new = _s5(u, h0, p['are_b'], p['aim_b'], p['bblk'], p['cblk'], p['d_row'], _tile(t, 128))
    y_att = _attn(q, k_att, v_att, cum, cum_rows, za, p['gq'], p['gk'], _tile(t, 512), attn_past)
    y = _out(x, mod3, y_ssm, zs, y_att, p['w_glu'], p['b_glu'], p['w_out'], _tile(t, 1024))
    return y, k_out, v_out, logf, h_new


def kernel(x_prompt, x_sample, cache_k, cache_v, cache_logf, state_ssm_re, state_ssm_im, c_prompt, c_sample,
           w_ada, b_ada, norm_g, w_in, b_f, q_norm_g, k_norm_g, ssm_log_dt, ssm_a_re, ssm_a_im,
           ssm_b_re, ssm_b_im, ssm_c_re, ssm_c_im, ssm_d, w_glu, b_glu, w_out):
    depth = w_ada.shape[0]
    d = x_prompt.shape[-1]
    bp = x_prompt.shape[0]
    bs = x_sample.shape[0]
    xp, xs = x_prompt, x_sample
    outs_p, outs_s = [], []
    for l in range(depth):
        mod = _mod(jnp.concatenate([c_prompt, c_sample], axis=0), w_ada[l], b_ada[l])
        mod3 = mod.reshape(bp + bs, 3, d)
        abar_re, abar_im, bbar_re, bbar_im = _zoh(ssm_log_dt[l], ssm_a_re[l], ssm_a_im[l],
                                                  ssm_b_re[l], ssm_b_im[l])
        a_lanes_re = _state_layout(abar_re, abar_re)
        a_lanes_im = _state_layout(abar_im, abar_im)
        split = 2 * SSM_WIDTH + 4 * ATTN_WIDTH
        hd = jnp.arange(ATTN_WIDTH) // HEAD_DIM
        p = dict(
            norm_g=norm_g[l].reshape(1, d),
            w_main=w_in[l][:, :split].astype(BF16),
            w_f=jnp.pad(w_in[l][:, split:], ((0, 0), (0, LANES - N_HEADS))).astype(BF16),
            b_f=jnp.pad(b_f[l], (0, LANES - N_HEADS)).reshape(1, LANES),
            gq=jnp.tile(q_norm_g[l], N_HEADS).reshape(1, ATTN_WIDTH),
            gk=jnp.tile(k_norm_g[l], N_HEADS).reshape(1, ATTN_WIDTH),
            e_mat=(hd[:, None] == hd[None, :]).astype(BF16) * (1.0 / HEAD_DIM),
            are_b=jnp.broadcast_to(a_lanes_re[None], (SUBLANES, STATE_LANES)),
            aim_b=jnp.broadcast_to(a_lanes_im[None], (SUBLANES, STATE_LANES)),
            bblk=_block_diag_b(bbar_re, bbar_im),
            cblk=_block_diag_c(ssm_c_re[l], ssm_c_im[l]),
            d_row=ssm_d[l].reshape(1, SSM_WIDTH),
            w_glu=w_glu[l].astype(BF16), b_glu=b_glu[l].reshape(1, SSM_WIDTH),
            w_out=w_out[l].astype(BF16),
        )
        h0_p = jnp.zeros((bp, STATE_LANES), F32)
        xp, k1, v1, f1, h1 = _layer(xp, mod3[:bp], h0_p, None, p)
        h0_s = _state_layout(state_ssm_re[l], state_ssm_im[l])
        xs, k2, v2, f2, h2 = _layer(xs, mod3[bp:], h0_s, (cache_k[l], cache_v[l], cache_logf[l]), p)
        outs_p.append((k1, v1, f1) + _state_unlayout(h1))
        outs_s.append((k2, v2, f2) + _state_unlayout(h2))

    def stack(items, i, shape_tail):
        arr = jnp.stack([it[i] for it in items])
        return arr.reshape(arr.shape[:3] + shape_tail) if shape_tail else arr

    hd_tail = (N_HEADS, HEAD_DIM)
    return (xp, xs,
            stack(outs_p, 0, hd_tail), stack(outs_p, 1, hd_tail), stack(outs_p, 2, ()),
            stack(outs_p, 3, ()), stack(outs_p, 4, ()),
            stack(outs_s, 0, hd_tail), stack(outs_s, 1, hd_tail), stack(outs_s, 2, ()),
            stack(outs_s, 3, ()), stack(outs_s, 4, ()))
```

```python
import functools

import numpy as np
import jax
import jax.numpy as jnp
from jax import lax
from jax.experimental import pallas as pl
from jax.experimental.pallas import tpu as pltpu

F32 = jnp.float32
BF16 = jnp.bfloat16

N_HEADS = 8
HEAD_DIM = 64
ATTN_WIDTH = N_HEADS * HEAD_DIM
SSM_GROUPS = 32
SSM_GROUP = 16
SSM_STATE = 64
SSM_WIDTH = SSM_GROUPS * SSM_GROUP
STATE_LANES = 2 * SSM_GROUPS * SSM_STATE
HALF_LANES = STATE_LANES // 2
HALF_CH = SSM_WIDTH // 2
ATTN_SCALE = HEAD_DIM ** -0.5
NORM_EPS = 1e-6
LANES = 128
SUBLANES = 8
MXU_WIDTH = 256
NEG_BIG = -1e30
LOG2E = float(np.log2(np.e))
GELU_C = float(np.float32(np.sqrt(2.0 / np.pi)))
VMEM_LIMIT = 56 * 1024 * 1024
SKIP_LOGIT_GAP = 105.0
BF16_NORM_SLACK = 1.02
S5_PHASES = 2
ATTN_SUB = 512
BOUNDED_LOGIT_RANGE = 96.0


def _cparams(sem):
    return pltpu.CompilerParams(dimension_semantics=sem, vmem_limit_bytes=VMEM_LIMIT)


def _sigmoid(x):
    return 1.0 / (1.0 + jnp.exp(-x))


def _silu(x):
    return x * _sigmoid(x)


def _split3(x):
    hi = x.astype(BF16)
    r1 = x - hi.astype(F32)
    mid = r1.astype(BF16)
    lo = (r1 - mid.astype(F32)).astype(BF16)
    return hi, mid, lo


def _tri(n):
    return jnp.tril(jnp.ones((n, n), F32)).astype(BF16)


def _mod_kernel(c_ref, w_ref, b_ref, o_ref):
    c = c_ref[...]
    o_ref[...] = jnp.dot(_silu(c), w_ref[...], preferred_element_type=F32) + b_ref[...]


def _mod(c, w_ada, b_ada):
    n, d = c.shape
    n_out = w_ada.shape[1]
    blk = 1024
    return pl.pallas_call(
        _mod_kernel,
        grid=(n_out // blk,),
        in_specs=[pl.BlockSpec((n, d), lambda j: (0, 0)),
                  pl.BlockSpec((d, blk), lambda j: (0, j)),
                  pl.BlockSpec((1, blk), lambda j: (0, j))],
        out_specs=pl.BlockSpec((n, blk), lambda j: (0, j)),
        out_shape=jax.ShapeDtypeStruct((n, n_out), F32),
        compiler_params=_cparams(("arbitrary",)),
        name="mod",
    )(c, w_ada, b_ada.reshape(1, n_out))


def _zoh_kernel(ldt_ref, are_ref, aim_ref, bre_ref, bim_ref,
                abre_ref, abim_ref, bbre_ref, bbim_ref):
    dt = jnp.exp(ldt_ref[...])
    a_re = are_ref[...]
    a_im = aim_ref[...]
    mag = jnp.exp(a_re * dt)
    ang = a_im * dt
    abar_re = mag * jnp.cos(ang)
    abar_im = mag * jnp.sin(ang)
    den = a_re * a_re + a_im * a_im
    n_re = abar_re - 1.0
    n_im = abar_im
    q_re = (n_re * a_re + n_im * a_im) / den
    q_im = (n_im * a_re - n_re * a_im) / den
    b_re = bre_ref[...]
    b_im = bim_ref[...]
    abre_ref[...] = abar_re
    abim_ref[...] = abar_im
    bbre_ref[...] = q_re * b_re - q_im * b_im
    bbim_ref[...] = q_re * b_im + q_im * b_re


def _zoh(log_dt, a_re, a_im, b_re, b_im):
    g, n, p = b_re.shape
    rows = g * n
    ldt = jnp.broadcast_to(log_dt[:, None], (g, n)).reshape(rows, 1)
    outs = pl.pallas_call(
        _zoh_kernel,
        out_shape=[jax.ShapeDtypeStruct((rows, 1), F32), jax.ShapeDtypeStruct((rows, 1), F32),
                   jax.ShapeDtypeStruct((rows, p), F32), jax.ShapeDtypeStruct((rows, p), F32)],
        name="zoh",
    )(ldt, a_re.reshape(rows, 1), a_im.reshape(rows, 1), b_re.reshape(rows, p), b_im.reshape(rows, p))
    abar_re, abar_im, bbar_re, bbar_im = outs
    return (abar_re.reshape(g, n), abar_im.reshape(g, n),
            bbar_re.reshape(g, n, p), bbar_im.reshape(g, n, p))


def _state_layout(x_re, x_im):
    lead = x_re.shape[:-2]
    hg = SSM_GROUPS // 2
    r = x_re.reshape(lead + (2, 1, hg * SSM_STATE))
    i = x_im.reshape(lead + (2, 1, hg * SSM_STATE))
    return jnp.concatenate([r, i], axis=-2).reshape(lead + (STATE_LANES,))


def _state_unlayout(h):
    lead = h.shape[:-1]
    hg = SSM_GROUPS // 2
    x = h.reshape(lead + (2, 2, hg, SSM_STATE))
    re = x[..., :, 0, :, :].reshape(lead + (SSM_GROUPS, SSM_STATE))
    im = x[..., :, 1, :, :].reshape(lead + (SSM_GROUPS, SSM_STATE))
    return re, im


def _block_diag_b(bbar_re, bbar_im):
    hg = SSM_GROUPS // 2
    rows, cols = hg * SSM_GROUP, hg * SSM_STATE
    own = (jnp.arange(rows) // SSM_GROUP)[:, None] == (jnp.arange(cols) // SSM_STATE)[None, :]
    out = []
    for hf in range(2):
        parts = []
        for bb in (bbar_re, bbar_im):
            b = jnp.transpose(bb[hf * hg:(hf + 1) * hg], (0, 2, 1)).reshape(rows, SSM_STATE)
            parts.append(jnp.where(own, jnp.tile(b, (1, hg)), 0.0))
        out.append(jnp.concatenate(parts, axis=1))
    return jnp.stack(out).astype(BF16)


def _block_diag_c(c_re, c_im):
    hg = SSM_GROUPS // 2
    rows, cols = hg * SSM_STATE, hg * SSM_GROUP
    own = (jnp.arange(rows) // SSM_STATE)[:, None] == (jnp.arange(cols) // SSM_GROUP)[None, :]
    out = []
    for hf in range(2):
        parts = []
        for cc in (c_re, -c_im):
            c = jnp.transpose(cc[hf * hg:(hf + 1) * hg], (0, 2, 1)).reshape(rows, SSM_GROUP)
            parts.append(jnp.where(own, jnp.tile(c, (1, hg)), 0.0))
        out.append(jnp.concatenate(parts, axis=0))
    return jnp.stack(out).astype(BF16)


def _inproj_kernel(x_ref, mod_ref, g_ref, wm_ref, wf_ref, bf_ref, gq_ref, gk_ref, e_ref, tri_ref, cinit_ref,
                   u_ref, zs_ref, q_ref, kout_ref, vout_ref, katt_ref, vatt_ref, za_ref, cum_ref, logf_ref,
                   *rest, rows):
    if rows:
        cum_rows_ref, carry_ref = rest
    else:
        carry_ref, = rest

    @pl.when(pl.program_id(1) == 0)
    def _():
        carry_ref[...] = cinit_ref[0]

    x = x_ref[0]
    ms = jnp.mean(x * x, axis=-1, keepdims=True)
    xn = x * lax.rsqrt(ms + NORM_EPS) * g_ref[...]
    shift = mod_ref[0, 0:1, :]
    scale = mod_ref[0, 1:2, :]
    hb = (xn * (1.0 + scale) + shift).astype(BF16)

    def sec(i):
        return jnp.dot(hb, wm_ref[:, i * 512:(i + 1) * 512], preferred_element_type=F32)

    def head_rmsnorm(y, g):
        y2 = (y * y).astype(BF16)
        ew = e_ref.shape[0]
        msq = jnp.concatenate([jnp.dot(y2[:, c:c + ew], e_ref[...], preferred_element_type=F32)
                               for c in range(0, y.shape[1], ew)], axis=-1)
        return y * lax.rsqrt(msq + NORM_EPS) * g

    u_ref[0] = sec(0)
    zs_ref[0] = sec(1).astype(BF16)
    q = head_rmsnorm(sec(2), gq_ref[...])
    q_ref[0] = (q * (ATTN_SCALE * LOG2E)).astype(BF16)
    k = head_rmsnorm(sec(3), gk_ref[...])
    kout_ref[0] = k
    v = sec(4)
    vout_ref[0] = v
    kb = k.astype(BF16)
    vb = v.astype(BF16)
    for h in range(N_HEADS):
        katt_ref[0, h] = kb[:, h * HEAD_DIM:(h + 1) * HEAD_DIM]
        vatt_ref[0, h] = vb[:, h * HEAD_DIM:(h + 1) * HEAD_DIM]
    za_ref[0] = sec(5).astype(BF16)
    f = jnp.dot(hb, wf_ref[...], preferred_element_type=F32) + bf_ref[...]
    logf = jnp.minimum(f, 0.0) - jnp.log1p(jnp.exp(-jnp.abs(f)))
    if rows:
        logf_ref[0] = logf.T[:N_HEADS, :]
    else:
        logf_ref[0] = logf[:, :N_HEADS]
    lane = lax.broadcasted_iota(jnp.int32, logf.shape, 1)
    hi, mid, lo = (part.astype(F32) for part in _split3(jnp.where(lane < N_HEADS, logf, 0.0)))
    packed = hi + pltpu.roll(mid, N_HEADS, 1) + pltpu.roll(lo, 2 * N_HEADS, 1)
    r = jnp.dot(tri_ref[...], packed.astype(BF16), preferred_element_type=F32)
    c = carry_ref[...] + r + pltpu.roll(r, LANES - N_HEADS, 1) + pltpu.roll(r, LANES - 2 * N_HEADS, 1)
    cum_ref[0] = c[:, :N_HEADS]
    if rows:
        cum_rows_ref[0] = c.T[:N_HEADS, :]
    tm = c.shape[0]
    carry_ref[...] = c[tm - 1:tm, :]


def _inproj(x, mod3, norm_g, w_main, w_f, b_f, gq, gk, e_mat, cum_init, tm):
    b, t, d = x.shape
    nt = t // tm
    tok = lambda w, dt: jax.ShapeDtypeStruct((b, t, w), dt)
    heads = jax.ShapeDtypeStruct((b, N_HEADS, t, HEAD_DIM), BF16)
    const = lambda shape: pl.BlockSpec(shape, lambda i, j: (0,) * len(shape))
    tokspec = lambda w: pl.BlockSpec((1, tm, w), lambda i, j: (i, j, 0))
    headspec = pl.BlockSpec((1, N_HEADS, tm, HEAD_DIM), lambda i, j: (i, 0, j, 0))
    rows = tm % LANES == 0
    rowspec = pl.BlockSpec((1, N_HEADS, tm), lambda i, j: (i, 0, j))
    rowshape = jax.ShapeDtypeStruct((b, N_HEADS, t), F32)
    out_specs = [tokspec(SSM_WIDTH), tokspec(SSM_WIDTH), tokspec(ATTN_WIDTH), tokspec(ATTN_WIDTH),
                 tokspec(ATTN_WIDTH), headspec, headspec, tokspec(ATTN_WIDTH), tokspec(N_HEADS)]
    out_shape = [tok(SSM_WIDTH, F32), tok(SSM_WIDTH, BF16), tok(ATTN_WIDTH, BF16), tok(ATTN_WIDTH, F32),
                 tok(ATTN_WIDTH, F32), heads, heads, tok(ATTN_WIDTH, BF16), tok(N_HEADS, F32)]
    out_specs += [rowspec, rowspec] if rows else [tokspec(N_HEADS)]
    out_shape += [rowshape, rowshape] if rows else [tok(N_HEADS, F32)]
    outs = pl.pallas_call(
        functools.partial(_inproj_kernel, rows=rows),
        grid=(b, nt),
        in_specs=[tokspec(d),
                  pl.BlockSpec((1, 3, d), lambda i, j: (i, 0, 0)),
                  const((1, d)), const(w_main.shape), const(w_f.shape), const((1, LANES)),
                  const((1, ATTN_WIDTH)), const((1, ATTN_WIDTH)), const(e_mat.shape), const((tm, tm)),
                  pl.BlockSpec((1, 1, LANES), lambda i, j: (i, 0, 0))],
        out_specs=out_specs,
        out_shape=out_shape,
        scratch_shapes=[pltpu.VMEM((1, LANES), F32)],
        compiler_params=_cparams(("arbitrary", "arbitrary")),
        name="inproj",
    )(x, mod3, norm_g, w_main, w_f, b_f, gq, gk, e_mat, _tri(tm), cum_init)
    main, cum = outs[:8], outs[8]
    if rows:
        logf_rows, cum_rows = outs[9:]
        logf = jnp.transpose(logf_rows, (0, 2, 1))
    else:
        logf = outs[9]
        cum_rows = jnp.transpose(cum, (0, 2, 1))
    return (*main, logf, cum, cum_rows)


def _cumsum_rows_kernel(x_ref, triu_ref, o_ref):
    tc = triu_ref.shape[0]
    total = jnp.zeros((x_ref.shape[1], 1), F32)
    for c0 in range(0, x_ref.shape[2], tc):
        c = total
        for part in _split3(x_ref[0, :, c0:c0 + tc]):
            c = c + jnp.dot(part, triu_ref[...], preferred_element_type=F32)
        o_ref[0, :, c0:c0 + tc] = c
        total = c[:, tc - 1:tc]


def _cumsum_rows(x, tc):
    b, h, t = x.shape
    return pl.pallas_call(
        _cumsum_rows_kernel,
        grid=(b,),
        in_specs=[pl.BlockSpec((1, h, t), lambda i: (i, 0, 0)),
                  pl.BlockSpec((tc, tc), lambda i: (0, 0))],
        out_specs=pl.BlockSpec((1, h, t), lambda i: (i, 0, 0)),
        out_shape=jax.ShapeDtypeStruct((b, h, t), F32),
        compiler_params=_cparams(("arbitrary",)),
        name="cumsum",
    )(x, jnp.triu(jnp.ones((tc, tc), F32)).astype(BF16))


def _s5_kernel(u_ref, h0_ref, are_ref, aim_ref, bblk_ref, cblk_ref, d_ref,
               y_ref, hout_ref,
               ut_ref, utm_ref, bu_ref, ytm_ref, *, tt, pitch):
    nb = u_ref.shape[0]
    nslab = SSM_WIDTH // LANES

    @pl.when(pl.program_id(0) == 0)
    def _():
        hout_ref[...] = h0_ref[...]

    for b in range(nb):
        for k in range(nslab):
            ut_ref[k, b * pitch:b * pitch + tt, :] = u_ref[b, :, k * LANES:(k + 1) * LANES]

    nph = S5_PHASES if tt % (S5_PHASES * SUBLANES) == 0 else 1
    tp = tt // nph
    chunk = 4 * LANES
    quarter = HALF_LANES // 2
    chunks = [(hf * HALF_LANES + c * chunk, hf * HALF_LANES + c * chunk + quarter)
              for hf in range(2) for c in range(quarter // chunk)]

    def b_proj(ph):
        for t in range(ph * tp, (ph + 1) * tp):
            for k in range(nslab):
                utm_ref[t * nb:(t + 1) * nb, k * LANES:(k + 1) * LANES] = ut_ref[k, pl.ds(t, nb, stride=pitch), :]
        rows = slice(ph * tp * nb, (ph + 1) * tp * nb)
        for hf in range(2):
            uh = utm_ref[rows, hf * HALF_CH:(hf + 1) * HALF_CH].astype(BF16)
            bu_ref[rows, hf * HALF_LANES:(hf + 1) * HALF_LANES] = jnp.dot(
                uh, bblk_ref[hf], preferred_element_type=F32)

    def scan(ph):
        state = [(hout_ref[:, lo_re:lo_re + chunk], hout_ref[:, lo_im:lo_im + chunk]) for lo_re, lo_im in chunks]
        for t in range(ph * tp, (ph + 1) * tp):
            r = slice(t * nb, (t + 1) * nb)
            for ci, (lo_re, lo_im) in enumerate(chunks):
                xr, xi = state[ci]
                ar = are_ref[:, lo_re:lo_re + chunk]
                ai = aim_ref[:, lo_re:lo_re + chunk]
                nr = ar * xr - ai * xi + bu_ref[r, lo_re:lo_re + chunk]
                ni = ar * xi + ai * xr + bu_ref[r, lo_im:lo_im + chunk]
                bu_ref[r, lo_re:lo_re + chunk] = nr
                bu_ref[r, lo_im:lo_im + chunk] = ni
                state[ci] = (nr, ni)
        for (lo_re, lo_im), (xr, xi) in zip(chunks, state):
            hout_ref[:, lo_re:lo_re + chunk] = xr
            hout_ref[:, lo_im:lo_im + chunk] = xi

    def c_proj(ph):
        rows = slice(ph * tp * nb, (ph + 1) * tp * nb)
        for hf in range(2):
            xh = bu_ref[rows, hf * HALF_LANES:(hf + 1) * HALF_LANES].astype(BF16)
            yh = jnp.dot(xh, cblk_ref[hf], preferred_element_type=F32)
            yh = yh + d_ref[:, hf * HALF_CH:(hf + 1) * HALF_CH] * utm_ref[rows, hf * HALF_CH:(hf + 1) * HALF_CH]
            for kk in range(HALF_CH // LANES):
                ytm_ref[hf * (HALF_CH // LANES) + kk, rows, :] = yh[:, kk * LANES:(kk + 1) * LANES]
        for b in range(nb):
            for k in range(nslab):
                y_ref[b, ph * tp:(ph + 1) * tp, k * LANES:(k + 1) * LANES] = (
                    ytm_ref[k, pl.ds(ph * tp * nb + b, tp, stride=nb), :]).astype(y_ref.dtype)

    b_proj(0)
    for ph in range(nph):
        if ph + 1 < nph:
            b_proj(ph + 1)
        scan(ph)
        c_proj(ph)


def _s5(u, h0, are_b, aim_b, bblk, cblk, d_row, tt):
    b, t, w = u.shape
    assert b == SUBLANES and t % tt == 0 and tt % SUBLANES == 0
    pitch = tt + SUBLANES
    rows = tt * b
    const = lambda shape: pl.BlockSpec(shape, lambda j: (0,) * len(shape))
    return pl.pallas_call(
        functools.partial(_s5_kernel, tt=tt, pitch=pitch),
        grid=(t // tt,),
        in_specs=[pl.BlockSpec((b, tt, w), lambda j: (0, j, 0)),
                  const((b, STATE_LANES)), const((b, STATE_LANES)), const((b, STATE_LANES)),
                  const(bblk.shape), const(cblk.shape), const((1, w))],
        out_specs=[pl.BlockSpec((b, tt, w), lambda j: (0, j, 0)), const((b, STATE_LANES))],
        out_shape=[jax.ShapeDtypeStruct((b, t, w), BF16), jax.ShapeDtypeStruct((b, STATE_LANES), F32)],
        scratch_shapes=[pltpu.VMEM((w // LANES, b * pitch, LANES), F32),
                        pltpu.VMEM((rows, w), F32),
                        pltpu.VMEM((rows, STATE_LANES), F32),
                        pltpu.VMEM((w // LANES, rows, LANES), F32)],
        compiler_params=_cparams(("arbitrary",)),
        name="s5",
    )(u, h0, are_b, aim_b, bblk, cblk, d_row)


def _attn_kernel(*refs, bq, has_past):
    if has_past:
        (q_ref, kc_ref, vc_ref, cq_ref, ckc_ref, cec_ref, gq_ref, gk_ref, za_ref,
         kp_ref, vp_ref, ckp_ref, cep_ref, o_ref) = refs
    else:
        q_ref, kc_ref, vc_ref, cq_ref, ckc_ref, cec_ref, gq_ref, gk_ref, za_ref, o_ref = refs
    hp = pl.program_id(1)
    heads_per_step = q_ref.shape[2] // HEAD_DIM
    heads = range(heads_per_step)
    nq = q_ref.shape[1] // bq

    smax = (jnp.max(jnp.abs(gq_ref[...]), axis=-1, keepdims=True)
            * jnp.max(jnp.abs(gk_ref[...]), axis=-1, keepdims=True)
            * (HEAD_DIM * ATTN_SCALE * BF16_NORM_SLACK * BF16_NORM_SLACK))
    skip_below = -(SKIP_LOGIT_GAP + 2.0 * smax)
    smax2 = smax * LOG2E

    def load_cur(ref, hh, start, n):
        return ref[0, hh, pl.ds(start, n), :]

    def load_past(ref, hh, start, n):
        return ref[0, pl.ds(start, n), hh * HEAD_DIM:(hh + 1) * HEAD_DIM].astype(BF16)

    cur = (load_cur, kc_ref, vc_ref, ckc_ref)
    past = (load_past, kp_ref, vp_ref, ckp_ref) if has_past else None

    def q_block(qi, online):
        row0 = pl.multiple_of(qi * bq, bq)
        qrows = pl.ds(row0, bq)
        cq_all = cq_ref[0, qrows, :]
        head_lane = lax.broadcasted_iota(jnp.int32, cq_all.shape, 1)
        qs = [q_ref[0, qrows, hh * HEAD_DIM:(hh + 1) * HEAD_DIM] for hh in heads]

        def cq_of(hh, rows):
            h = hp * heads_per_step + hh
            return jnp.sum(jnp.where(head_lane[rows] == h, cq_all[rows], 0.0), axis=-1, keepdims=True)

        def first_live(ce_ref, n_visible):
            dead = None
            for hh in heads:
                cq_first = cq_of(hh, slice(0, 1))
                ce = ce_ref[0, hh]
                idx = lax.broadcasted_iota(jnp.int32, ce.shape, 1)
                d = jnp.logical_and(cq_first - ce < skip_below, idx < n_visible)
                dead = d if dead is None else jnp.logical_and(dead, d)
            return jnp.sum(dead.astype(jnp.int32))

        def finish(outs):
            o = jnp.concatenate(outs, axis=-1)
            o_ref[0, qrows, :] = (o * _silu(za_ref[0, qrows, :].astype(F32))).astype(BF16)

        def sweep(block, src, lo, hi, carry):
            odd = lo + jnp.bitwise_and(hi - lo, 1)
            carry = lax.fori_loop(lo, odd, lambda j, c: block(src, j, c), carry)

            def pair(i, c):
                j = odd + 2 * i
                return block(src, j + 1, block(src, j, c))
            return lax.fori_loop(0, lax.shift_right_logical(hi - odd, 1), pair, carry)

        def all_blocks(block, carry):
            if has_past:
                npb = ckp_ref.shape[2]
                carry = sweep(block, past, first_live(cep_ref, npb), npb, carry)
            return sweep(block, cur, first_live(cec_ref, qi), qi, carry)

        if not online:
            refs_row = [smax2 - cq_of(hh, slice(None)) * LOG2E for hh in heads]

            def piece(src, hh, j, off, sub, causal, carry_h):
                load, k_ref, v_ref, ck_ref = src
                lp, acc = carry_h
                bk = ck_ref.shape[4]
                start = pl.multiple_of(j * bk + off, sub)
                s = lax.dot_general(qs[hh], load(k_ref, hh, start, sub), (((1,), (1,)), ((), ())),
                                    preferred_element_type=F32)
                t = s - ck_ref[0, hh, j][:, off:off + sub] * LOG2E - refs_row[hh]
                if causal:
                    r = lax.broadcasted_iota(jnp.int32, t.shape, 0)
                    c = lax.broadcasted_iota(jnp.int32, t.shape, 1) + off
                    t = jnp.where(r >= c, t, NEG_BIG)
                p = jnp.exp2(t)
                if sub < LANES:
                    psum = jnp.concatenate([p, jnp.zeros((p.shape[0], LANES - sub), F32)], axis=-1)
                else:
                    psum = p[:, 0:LANES]
                    for c0 in range(LANES, sub, LANES):
                        psum = psum + p[:, c0:c0 + LANES]
                pv = jnp.dot(p.astype(BF16), load(v_ref, hh, start, sub), preferred_element_type=F32)
                return lp + psum, acc + pv

            def block(src, j, carry):
                bk = src[3].shape[4]
                sub = min(bk, ATTN_SUB)
                new = []
                for hh in heads:
                    c = carry[hh]
                    for off in range(0, bk, sub):
                        c = piece(src, hh, j, off, sub, False, c)
                    new.append(c)
                return tuple(new)

            carry = tuple((jnp.zeros((bq, LANES), F32), jnp.zeros((bq, HEAD_DIM), F32)) for _ in heads)
            carry = all_blocks(block, carry)
            outs = []
            for hh in heads:
                lp, acc = piece(cur, hh, qi, 0, bq, True, carry[hh])
                outs.append(acc / jnp.sum(lp, axis=-1, keepdims=True))
            finish(outs)
        else:
            def update(s, v, carry_h):
                m, l, acc = carry_h
                m_new = jnp.maximum(m, jnp.max(s, axis=-1, keepdims=True))
                alpha = jnp.exp2(m - m_new)
                p = jnp.exp2(s - m_new)
                l = alpha * l + jnp.sum(p, axis=-1, keepdims=True)
                acc = alpha * acc + jnp.dot(p.astype(BF16), v, preferred_element_type=F32)
                return m_new, l, acc

            def scores(src, hh, j):
                load, k_ref, _, ck_ref = src
                bk = ck_ref.shape[4]
                start = pl.multiple_of(j * bk, bk)
                s = lax.dot_general(qs[hh], load(k_ref, hh, start, bk), (((1,), (1,)), ((), ())),
                                    preferred_element_type=F32)
                return s - ck_ref[0, hh, j] * LOG2E, start, bk

            def block(src, j, carry):
                new = []
                for hh in heads:
                    s, start, bk = scores(src, hh, j)
                    new.append(update(s, src[0](src[2], hh, start, bk), carry[hh]))
                return tuple(new)

            carry = tuple((jnp.full((bq, 1), NEG_BIG, F32), jnp.zeros((bq, 1), F32),
                           jnp.zeros((bq, HEAD_DIM), F32)) for _ in heads)
            carry = all_blocks(block, carry)
            row = lax.broadcasted_iota(jnp.int32, (bq, bq), 0)
            col = lax.broadcasted_iota(jnp.int32, (bq, bq), 1)
            outs = []
            for hh in heads:
                s, start, bk = scores(cur, hh, qi)
                s = jnp.where(row >= col, s, NEG_BIG)
                _, l, acc = update(s, load_cur(vc_ref, hh, start, bk), carry[hh])
                outs.append(acc / l)
            finish(outs)
        return 0

    small_logits = jnp.max(smax2) * 2.0 <= BOUNDED_LOGIT_RANGE

    @pl.when(small_logits)
    def _():
        lax.fori_loop(0, nq, lambda qi, _: q_block(qi, False), 0)

    @pl.when(jnp.logical_not(small_logits))
    def _():
        lax.fori_loop(0, nq, lambda qi, _: q_block(qi, True), 0)


def _row_blocks(cum_rows, blk):
    b, h, t = cum_rows.shape
    rows = cum_rows.reshape(b, h, t // blk, 1, blk)
    return rows, rows[:, :, :, 0, blk - 1].reshape(b, h, 1, t // blk)


def _attn(q, k_cur, v_cur, cum, cum_rows, za, gq, gk, bq, past=None):
    b, tq, _ = q.shape
    hps = LANES // HEAD_DIM
    nq = tq // bq
    ckc, cec = _row_blocks(cum_rows, bq)
    qspec = pl.BlockSpec((1, tq, LANES), lambda i, h: (i, 0, h))
    kvspec = pl.BlockSpec((1, hps, tq, HEAD_DIM), lambda i, h: (i, h, 0, 0))
    rowspec = lambda n, w: pl.BlockSpec((1, hps, n, 1, w), lambda i, h: (i, h, 0, 0, 0))
    endspec = lambda n: pl.BlockSpec((1, hps, 1, n), lambda i, h: (i, h, 0, 0))
    gspec = pl.BlockSpec((1, ATTN_WIDTH), lambda i, h: (0, 0))
    in_specs = [qspec, kvspec, kvspec, pl.BlockSpec((1, tq, N_HEADS), lambda i, h: (i, 0, 0)),
                rowspec(nq, bq), endspec(nq), gspec, gspec, qspec]
    args = [q, k_cur, v_cur, cum, ckc, cec, gq, gk, za]
    if past is not None:
        k_past, v_past, cum_past_rows, bkp = past
        tp = k_past.shape[1]
        ckp, cep = _row_blocks(cum_past_rows, bkp)
        pastspec = pl.BlockSpec((1, tp, LANES), lambda i, h: (i, 0, h))
        in_specs += [pastspec, pastspec, rowspec(tp // bkp, bkp), endspec(tp // bkp)]
        args += [k_past, v_past, ckp, cep]
    return pl.pallas_call(
        functools.partial(_attn_kernel, bq=bq, has_past=past is not None),
        grid=(b, N_HEADS // hps),
        in_specs=in_specs,
        out_specs=qspec,
        out_shape=jax.ShapeDtypeStruct((b, tq, ATTN_WIDTH), BF16),
        compiler_params=_cparams(("arbitrary", "arbitrary")),
        name="attn",
    )(*args)


def _out_kernel(x_ref, mod_ref, ys_ref, zs_ref, ya_ref, wglu_ref, bglu_ref, wo_ref, o_ref):
    y = ys_ref[0].astype(F32)
    g = y * (0.5 * (1.0 + jnp.tanh(GELU_C * (y + 0.044715 * (y * y * y)))))
    glu = jnp.dot(g.astype(BF16), wglu_ref[...], preferred_element_type=F32) + bglu_ref[...]
    y1 = g * _sigmoid(glu) * _silu(zs_ref[0].astype(F32))
    mixed = (jnp.dot(y1.astype(BF16), wo_ref[0:SSM_WIDTH, :], preferred_element_type=F32)
             + jnp.dot(ya_ref[0], wo_ref[SSM_WIDTH:, :], preferred_element_type=F32))
    o_ref[0] = x_ref[0] + mod_ref[0, 2:3, :] * mixed


def _out(x, mod3, y_ssm, zs, y_att, w_glu, b_glu, w_out, tm):
    b, t, d = x.shape
    const = lambda shape: pl.BlockSpec(shape, lambda i, j: (0,) * len(shape))
    tokspec = lambda w: pl.BlockSpec((1, tm, w), lambda i, j: (i, j, 0))
    return pl.pallas_call(
        _out_kernel,
        grid=(b, t // tm),
        in_specs=[tokspec(d), pl.BlockSpec((1, 3, d), lambda i, j: (i, 0, 0)),
                  tokspec(SSM_WIDTH), tokspec(SSM_WIDTH), tokspec(ATTN_WIDTH),
                  const(w_glu.shape), const((1, SSM_WIDTH)), const(w_out.shape)],
        out_specs=tokspec(d),
        out_shape=jax.ShapeDtypeStruct((b, t, d), F32),
        compiler_params=_cparams(("arbitrary", "arbitrary")),
        name="out",
    )(x, mod3, y_ssm, zs, y_att, w_glu, b_glu, w_out)


def _tile(t, pref):
    return pref if t % pref == 0 else t


def _layer(x, mod3, h0, past, p):
    b, t, d = x.shape
    tm = _tile(t, 512)
    if past is None:
        cum_init = jnp.zeros((b, 1, LANES), F32)
        attn_past = None
    else:
        ck_past, cv_past, clogf = past
        tp = ck_past.shape[1]
        cum_past_rows = _cumsum_rows(jnp.transpose(clogf.astype(F32), (0, 2, 1)), _tile(tp, 512))
        cum_init = jnp.pad(cum_past_rows[:, :, tp - 1], ((0, 0), (0, LANES - N_HEADS))).reshape(b, 1, LANES)
        attn_past = (ck_past.reshape(b, tp, ATTN_WIDTH), cv_past.reshape(b, tp, ATTN_WIDTH),
                     cum_past_rows, _tile(tp, 512))

    u, zs, q, k_out, v_out, k_att, v_att, za, logf, cum, cum_rows = _inproj(
        x, mod3, p['norm_g'], p['w_main'], p['w_f'], p['b_f'], p['gq'], p['gk'], p['e_mat'], cum_init, tm)
    y_ssm, h_new = _s5(u, h0, p['are_b'], p['aim_b'], p['bblk'], p['cblk'], p['d_row'], _tile(t, 128))
    y_att = _attn(q, k_att, v_att, cum, cum_rows, za, p['gq'], p['gk'], _tile(t, 512), attn_past)
    y = _out(x, mod3, y_ssm, zs, y_att, p['w_glu'], p['b_glu'], p['w_out'], _tile(t, 1024))
    return y, k_out, v_out, logf, h_new


def kernel(x_prompt, x_sample, cache_k, cache_v, cache_logf, state_ssm_re, state_ssm_im, c_prompt, c_sample,
           w_ada, b_ada, norm_g, w_in, b_f, q_norm_g, k_norm_g, ssm_log_dt, ssm_a_re, ssm_a_im,
           ssm_b_re, ssm_b_im, ssm_c_re, ssm_c_im, ssm_d, w_glu, b_glu, w_out):
    depth = w_ada.shape[0]
    d = x_prompt.shape[-1]
    bp = x_prompt.shape[0]
    bs = x_sample.shape[0]
    xp, xs = x_prompt, x_sample
    outs_p, outs_s = [], []
    for l in range(depth):
        mod = _mod(jnp.concatenate([c_prompt, c_sample], axis=0), w_ada[l], b_ada[l])
        mod3 = mod.reshape(bp + bs, 3, d)
        abar_re, abar_im, bbar_re, bbar_im = _zoh(ssm_log_dt[l], ssm_a_re[l], ssm_a_im[l],
                                                  ssm_b_re[l], ssm_b_im[l])
        a_lanes_re = _state_layout(abar_re, abar_re)
        a_lanes_im = _state_layout(abar_im, abar_im)
        split = 2 * SSM_WIDTH + 4 * ATTN_WIDTH
        hd = jnp.arange(MXU_WIDTH) // HEAD_DIM
        p = dict(
            norm_g=norm_g[l].reshape(1, d),
            w_main=w_in[l][:, :split].astype(BF16),
            w_f=jnp.pad(w_in[l][:, split:], ((0, 0), (0, LANES - N_HEADS))).astype(BF16),
            b_f=jnp.pad(b_f[l], (0, LANES - N_HEADS)).reshape(1, LANES),
            gq=jnp.tile(q_norm_g[l], N_HEADS).reshape(1, ATTN_WIDTH),
            gk=jnp.tile(k_norm_g[l], N_HEADS).reshape(1, ATTN_WIDTH),
            e_mat=(hd[:, None] == hd[None, :]).astype(BF16) * (1.0 / HEAD_DIM),
            are_b=jnp.broadcast_to(a_lanes_re[None], (SUBLANES, STATE_LANES)),
            aim_b=jnp.broadcast_to(a_lanes_im[None], (SUBLANES, STATE_LANES)),
            bblk=_block_diag_b(bbar_re, bbar_im),
            cblk=_block_diag_c(ssm_c_re[l], ssm_c_im[l]),
            d_row=ssm_d[l].reshape(1, SSM_WIDTH),
            w_glu=w_glu[l].astype(BF16), b_glu=b_glu[l].reshape(1, SSM_WIDTH),
            w_out=w_out[l].astype(BF16),
        )
        h0_p = jnp.zeros((bp, STATE_LANES), F32)
        xp, k1, v1, f1, h1 = _layer(xp, mod3[:bp], h0_p, None, p)
        h0_s = _state_layout(state_ssm_re[l], state_ssm_im[l])
        xs, k2, v2, f2, h2 = _layer(xs, mod3[bp:], h0_s, (cache_k[l], cache_v[l], cache_logf[l]), p)
        outs_p.append((k1, v1, f1) + _state_unlayout(h1))
        outs_s.append((k2, v2, f2) + _state_unlayout(h2))

    def stack(items, i, shape_tail):
        arr = jnp.stack([it[i] for it in items])
        return arr.reshape(arr.shape[:3] + shape_tail) if shape_tail else arr

    hd_tail = (N_HEADS, HEAD_DIM)
    return (xp, xs,
            stack(outs_p, 0, hd_tail), stack(outs_p, 1, hd_tail), stack(outs_p, 2, ()),
            stack(outs_p, 3, ()), stack(outs_p, 4, ()),
            stack(outs_s, 0, hd_tail), stack(outs_s, 1, hd_tail), stack(outs_s, 2, ()),
            stack(outs_s, 3, ()), stack(outs_s, 4, ()))
```

```python
import functools

import numpy as np
import jax
import jax.numpy as jnp
from jax import lax
from jax.experimental import pallas as pl
from jax.experimental.pallas import tpu as pltpu

F32 = jnp.float32
BF16 = jnp.bfloat16

N_HEADS = 8
HEAD_DIM = 64
ATTN_WIDTH = N_HEADS * HEAD_DIM
SSM_GROUPS = 32
SSM_GROUP = 16
SSM_STATE = 64
SSM_WIDTH = SSM_GROUPS * SSM_GROUP
STATE_LANES = 2 * SSM_GROUPS * SSM_STATE
HALF_LANES = STATE_LANES // 2
HALF_CH = SSM_WIDTH // 2
ATTN_SCALE = HEAD_DIM ** -0.5
NORM_EPS = 1e-6
LANES = 128
SUBLANES = 8
MXU_WIDTH = 256
NEG_BIG = -1e30
LOG2E = float(np.log2(np.e))
GELU_C = float(np.float32(np.sqrt(2.0 / np.pi)))
VMEM_LIMIT = 58 * 1024 * 1024
SKIP_LOGIT_GAP = 105.0
BF16_NORM_SLACK = 1.02
S5_PHASES = 2
ATTN_SUB = 512
BOUNDED_LOGIT_RANGE = 96.0


def _cparams(sem):
    return pltpu.CompilerParams(dimension_semantics=sem, vmem_limit_bytes=VMEM_LIMIT)


def _sigmoid(x):
    return 1.0 / (1.0 + jnp.exp(-x))


def _silu(x):
    return x * _sigmoid(x)


def _split3(x):
    hi = x.astype(BF16)
    r1 = x - hi.astype(F32)
    mid = r1.astype(BF16)
    lo = (r1 - mid.astype(F32)).astype(BF16)
    return hi, mid, lo


def _tri(n):
    return jnp.tril(jnp.ones((n, n), F32)).astype(BF16)


def _mod_kernel(c_ref, w_ref, b_ref, o_ref):
    c = c_ref[...]
    o_ref[...] = jnp.dot(_silu(c), w_ref[...], preferred_element_type=F32) + b_ref[...]


def _mod(c, w_ada, b_ada):
    n, d = c.shape
    n_out = w_ada.shape[1]
    blk = 1024
    return pl.pallas_call(
        _mod_kernel,
        grid=(n_out // blk,),
        in_specs=[pl.BlockSpec((n, d), lambda j: (0, 0)),
                  pl.BlockSpec((d, blk), lambda j: (0, j)),
                  pl.BlockSpec((1, blk), lambda j: (0, j))],
        out_specs=pl.BlockSpec((n, blk), lambda j: (0, j)),
        out_shape=jax.ShapeDtypeStruct((n, n_out), F32),
        compiler_params=_cparams(("arbitrary",)),
        name="mod",
    )(c, w_ada, b_ada.reshape(1, n_out))


def _zoh_kernel(ldt_ref, are_ref, aim_ref, bre_ref, bim_ref,
                abre_ref, abim_ref, bbre_ref, bbim_ref):
    dt = jnp.exp(ldt_ref[...])
    a_re = are_ref[...]
    a_im = aim_ref[...]
    mag = jnp.exp(a_re * dt)
    ang = a_im * dt
    abar_re = mag * jnp.cos(ang)
    abar_im = mag * jnp.sin(ang)
    den = a_re * a_re + a_im * a_im
    n_re = abar_re - 1.0
    n_im = abar_im
    q_re = (n_re * a_re + n_im * a_im) / den
    q_im = (n_im * a_re - n_re * a_im) / den
    b_re = bre_ref[...]
    b_im = bim_ref[...]
    abre_ref[...] = abar_re
    abim_ref[...] = abar_im
    bbre_ref[...] = q_re * b_re - q_im * b_im
    bbim_ref[...] = q_re * b_im + q_im * b_re


def _zoh(log_dt, a_re, a_im, b_re, b_im):
    g, n, p = b_re.shape
    rows = g * n
    ldt = jnp.broadcast_to(log_dt[:, None], (g, n)).reshape(rows, 1)
    outs = pl.pallas_call(
        _zoh_kernel,
        out_shape=[jax.ShapeDtypeStruct((rows, 1), F32), jax.ShapeDtypeStruct((rows, 1), F32),
                   jax.ShapeDtypeStruct((rows, p), F32), jax.ShapeDtypeStruct((rows, p), F32)],
        name="zoh",
    )(ldt, a_re.reshape(rows, 1), a_im.reshape(rows, 1), b_re.reshape(rows, p), b_im.reshape(rows, p))
    abar_re, abar_im, bbar_re, bbar_im = outs
    return (abar_re.reshape(g, n), abar_im.reshape(g, n),
            bbar_re.reshape(g, n, p), bbar_im.reshape(g, n, p))


def _state_layout(x_re, x_im):
    lead = x_re.shape[:-2]
    hg = SSM_GROUPS // 2
    r = x_re.reshape(lead + (2, 1, hg * SSM_STATE))
    i = x_im.reshape(lead + (2, 1, hg * SSM_STATE))
    return jnp.concatenate([r, i], axis=-2).reshape(lead + (STATE_LANES,))


def _state_unlayout(h):
    lead = h.shape[:-1]
    hg = SSM_GROUPS // 2
    x = h.reshape(lead + (2, 2, hg, SSM_STATE))
    re = x[..., :, 0, :, :].reshape(lead + (SSM_GROUPS, SSM_STATE))
    im = x[..., :, 1, :, :].reshape(lead + (SSM_GROUPS, SSM_STATE))
    return re, im


def _block_diag_b(bbar_re, bbar_im):
    hg = SSM_GROUPS // 2
    rows, cols = hg * SSM_GROUP, hg * SSM_STATE
    own = (jnp.arange(rows) // SSM_GROUP)[:, None] == (jnp.arange(cols) // SSM_STATE)[None, :]
    out = []
    for hf in range(2):
        parts = []
        for bb in (bbar_re, bbar_im):
            b = jnp.transpose(bb[hf * hg:(hf + 1) * hg], (0, 2, 1)).reshape(rows, SSM_STATE)
            parts.append(jnp.where(own, jnp.tile(b, (1, hg)), 0.0))
        out.append(jnp.concatenate(parts, axis=1))
    return jnp.stack(out).astype(BF16)


def _block_diag_c(c_re, c_im):
    hg = SSM_GROUPS // 2
    rows, cols = hg * SSM_STATE, hg * SSM_GROUP
    own = (jnp.arange(rows) // SSM_STATE)[:, None] == (jnp.arange(cols) // SSM_GROUP)[None, :]
    out = []
    for hf in range(2):
        parts = []
        for cc in (c_re, -c_im):
            c = jnp.transpose(cc[hf * hg:(hf + 1) * hg], (0, 2, 1)).reshape(rows, SSM_GROUP)
            parts.append(jnp.where(own, jnp.tile(c, (1, hg)), 0.0))
        out.append(jnp.concatenate(parts, axis=0))
    return jnp.stack(out).astype(BF16)


def _inproj_kernel(x_ref, mod_ref, g_ref, wm_ref, wf_ref, bf_ref, gq_ref, gk_ref, e_ref, tri_ref, cinit_ref,
                   u_ref, zs_ref, q_ref, kout_ref, vout_ref, katt_ref, vatt_ref, za_ref, cum_ref, logf_ref,
                   *rest, rows):
    if rows:
        cum_rows_ref, carry_ref = rest
    else:
        carry_ref, = rest

    @pl.when(pl.program_id(1) == 0)
    def _():
        carry_ref[...] = cinit_ref[0]

    x = x_ref[0]
    ms = jnp.mean(x * x, axis=-1, keepdims=True)
    xn = x * lax.rsqrt(ms + NORM_EPS) * g_ref[...]
    shift = mod_ref[0, 0:1, :]
    scale = mod_ref[0, 1:2, :]
    hb = (xn * (1.0 + scale) + shift).astype(BF16)

    def sec(i):
        return jnp.dot(hb, wm_ref[:, i * 512:(i + 1) * 512], preferred_element_type=F32)

    def head_rmsnorm(y, g):
        y2 = (y * y).astype(BF16)
        ew = e_ref.shape[0]
        msq = jnp.concatenate([jnp.dot(y2[:, c:c + ew], e_ref[...], preferred_element_type=F32)
                               for c in range(0, y.shape[1], ew)], axis=-1)
        return y * lax.rsqrt(msq + NORM_EPS) * g

    u_ref[0] = sec(0)
    zs_ref[0] = sec(1).astype(BF16)
    q = head_rmsnorm(sec(2), gq_ref[...])
    q_ref[0] = (q * (ATTN_SCALE * LOG2E)).astype(BF16)
    k = head_rmsnorm(sec(3), gk_ref[...])
    kout_ref[0] = k
    v = sec(4)
    vout_ref[0] = v
    kb = k.astype(BF16)
    vb = v.astype(BF16)
    for h in range(N_HEADS):
        katt_ref[0, h] = kb[:, h * HEAD_DIM:(h + 1) * HEAD_DIM]
        vatt_ref[0, h] = vb[:, h * HEAD_DIM:(h + 1) * HEAD_DIM]
    za_ref[0] = sec(5).astype(BF16)
    f = jnp.dot(hb, wf_ref[...], preferred_element_type=F32) + bf_ref[...]
    logf = jnp.minimum(f, 0.0) - jnp.log1p(jnp.exp(-jnp.abs(f)))
    if rows:
        logf_ref[0] = logf.T[:N_HEADS, :]
    else:
        logf_ref[0] = logf[:, :N_HEADS]
    lane = lax.broadcasted_iota(jnp.int32, logf.shape, 1)
    hi, mid, lo = (part.astype(F32) for part in _split3(jnp.where(lane < N_HEADS, logf, 0.0)))
    packed = hi + pltpu.roll(mid, N_HEADS, 1) + pltpu.roll(lo, 2 * N_HEADS, 1)
    r = jnp.dot(tri_ref[...], packed.astype(BF16), preferred_element_type=F32)
    c = carry_ref[...] + r + pltpu.roll(r, LANES - N_HEADS, 1) + pltpu.roll(r, LANES - 2 * N_HEADS, 1)
    cum_ref[0] = c[:, :N_HEADS]
    if rows:
        cum_rows_ref[0] = c.T[:N_HEADS, :]
    tm = c.shape[0]
    carry_ref[...] = c[tm - 1:tm, :]


def _inproj(x, mod3, norm_g, w_main, w_f, b_f, gq, gk, e_mat, cum_init, tm):
    b, t, d = x.shape
    nt = t // tm
    tok = lambda w, dt: jax.ShapeDtypeStruct((b, t, w), dt)
    heads = jax.ShapeDtypeStruct((b, N_HEADS, t, HEAD_DIM), BF16)
    const = lambda shape: pl.BlockSpec(shape, lambda i, j: (0,) * len(shape))
    tokspec = lambda w: pl.BlockSpec((1, tm, w), lambda i, j: (i, j, 0))
    headspec = pl.BlockSpec((1, N_HEADS, tm, HEAD_DIM), lambda i, j: (i, 0, j, 0))
    rows = tm % LANES == 0
    rowspec = pl.BlockSpec((1, N_HEADS, tm), lambda i, j: (i, 0, j))
    rowshape = jax.ShapeDtypeStruct((b, N_HEADS, t), F32)
    out_specs = [tokspec(SSM_WIDTH), tokspec(SSM_WIDTH), tokspec(ATTN_WIDTH), tokspec(ATTN_WIDTH),
                 tokspec(ATTN_WIDTH), headspec, headspec, tokspec(ATTN_WIDTH), tokspec(N_HEADS)]
    out_shape = [tok(SSM_WIDTH, F32), tok(SSM_WIDTH, BF16), tok(ATTN_WIDTH, BF16), tok(ATTN_WIDTH, F32),
                 tok(ATTN_WIDTH, F32), heads, heads, tok(ATTN_WIDTH, BF16), tok(N_HEADS, F32)]
    out_specs += [rowspec, rowspec] if rows else [tokspec(N_HEADS)]
    out_shape += [rowshape, rowshape] if rows else [tok(N_HEADS, F32)]
    outs = pl.pallas_call(
        functools.partial(_inproj_kernel, rows=rows),
        grid=(b, nt),
        in_specs=[tokspec(d),
                  pl.BlockSpec((1, 3, d), lambda i, j: (i, 0, 0)),
                  const((1, d)), const(w_main.shape), const(w_f.shape), const((1, LANES)),
                  const((1, ATTN_WIDTH)), const((1, ATTN_WIDTH)), const(e_mat.shape), const((tm, tm)),
                  pl.BlockSpec((1, 1, LANES), lambda i, j: (i, 0, 0))],
        out_specs=out_specs,
        out_shape=out_shape,
        scratch_shapes=[pltpu.VMEM((1, LANES), F32)],
        compiler_params=_cparams(("arbitrary", "arbitrary")),
        name="inproj",
    )(x, mod3, norm_g, w_main, w_f, b_f, gq, gk, e_mat, _tri(tm), cum_init)
    main, cum = outs[:8], outs[8]
    if rows:
        logf_rows, cum_rows = outs[9:]
        logf = jnp.transpose(logf_rows, (0, 2, 1))
    else:
        logf = outs[9]
        cum_rows = jnp.transpose(cum, (0, 2, 1))
    return (*main, logf, cum, cum_rows)


def _cumsum_rows_kernel(x_ref, triu_ref, o_ref):
    tc = triu_ref.shape[0]
    total = jnp.zeros((x_ref.shape[1], 1), F32)
    for c0 in range(0, x_ref.shape[2], tc):
        c = total
        for part in _split3(x_ref[0, :, c0:c0 + tc]):
            c = c + jnp.dot(part, triu_ref[...], preferred_element_type=F32)
        o_ref[0, :, c0:c0 + tc] = c
        total = c[:, tc - 1:tc]


def _cumsum_rows(x, tc):
    b, h, t = x.shape
    return pl.pallas_call(
        _cumsum_rows_kernel,
        grid=(b,),
        in_specs=[pl.BlockSpec((1, h, t), lambda i: (i, 0, 0)),
                  pl.BlockSpec((tc, tc), lambda i: (0, 0))],
        out_specs=pl.BlockSpec((1, h, t), lambda i: (i, 0, 0)),
        out_shape=jax.ShapeDtypeStruct((b, h, t), F32),
        compiler_params=_cparams(("arbitrary",)),
        name="cumsum",
    )(x, jnp.triu(jnp.ones((tc, tc), F32)).astype(BF16))


def _s5_kernel(u_ref, h0_ref, are_ref, aim_ref, bblk_ref, cblk_ref, d_ref,
               y_ref, hout_ref,
               ut_ref, utm_ref, bu_ref, ytm_ref, *, tt, pitch):
    nb = u_ref.shape[0]
    nslab = SSM_WIDTH // LANES

    @pl.when(pl.program_id(0) == 0)
    def _():
        hout_ref[...] = h0_ref[...]

    for b in range(nb):
        for k in range(nslab):
            ut_ref[k, b * pitch:b * pitch + tt, :] = u_ref[b, :, k * LANES:(k + 1) * LANES]

    nph = S5_PHASES if tt % (S5_PHASES * SUBLANES) == 0 else 1
    tp = tt // nph
    chunk = 4 * LANES
    quarter = HALF_LANES // 2
    chunks = [(hf * HALF_LANES + c * chunk, hf * HALF_LANES + c * chunk + quarter)
              for hf in range(2) for c in range(quarter // chunk)]

    def b_proj(ph):
        for t in range(ph * tp, (ph + 1) * tp):
            for k in range(nslab):
                utm_ref[t * nb:(t + 1) * nb, k * LANES:(k + 1) * LANES] = ut_ref[k, pl.ds(t, nb, stride=pitch), :]
        rows = slice(ph * tp * nb, (ph + 1) * tp * nb)
        for hf in range(2):
            uh = utm_ref[rows, hf * HALF_CH:(hf + 1) * HALF_CH].astype(BF16)
            bu_ref[rows, hf * HALF_LANES:(hf + 1) * HALF_LANES] = jnp.dot(
                uh, bblk_ref[hf], preferred_element_type=F32)

    def scan(ph):
        state = [(hout_ref[:, lo_re:lo_re + chunk], hout_ref[:, lo_im:lo_im + chunk]) for lo_re, lo_im in chunks]
        for t in range(ph * tp, (ph + 1) * tp):
            r = slice(t * nb, (t + 1) * nb)
            for ci, (lo_re, lo_im) in enumerate(chunks):
                xr, xi = state[ci]
                ar = are_ref[:, lo_re:lo_re + chunk]
                ai = aim_ref[:, lo_re:lo_re + chunk]
                nr = ar * xr - ai * xi + bu_ref[r, lo_re:lo_re + chunk]
                ni = ar * xi + ai * xr + bu_ref[r, lo_im:lo_im + chunk]
                bu_ref[r, lo_re:lo_re + chunk] = nr
                bu_ref[r, lo_im:lo_im + chunk] = ni
                state[ci] = (nr, ni)
        for (lo_re, lo_im), (xr, xi) in zip(chunks, state):
            hout_ref[:, lo_re:lo_re + chunk] = xr
            hout_ref[:, lo_im:lo_im + chunk] = xi

    def c_proj(ph):
        rows = slice(ph * tp * nb, (ph + 1) * tp * nb)
        for hf in range(2):
            xh = bu_ref[rows, hf * HALF_LANES:(hf + 1) * HALF_LANES].astype(BF16)
            yh = jnp.dot(xh, cblk_ref[hf], preferred_element_type=F32)
            yh = yh + d_ref[:, hf * HALF_CH:(hf + 1) * HALF_CH] * utm_ref[rows, hf * HALF_CH:(hf + 1) * HALF_CH]
            for kk in range(HALF_CH // LANES):
                ytm_ref[hf * (HALF_CH // LANES) + kk, rows, :] = yh[:, kk * LANES:(kk + 1) * LANES]
        for b in range(nb):
            for k in range(nslab):
                y_ref[b, ph * tp:(ph + 1) * tp, k * LANES:(k + 1) * LANES] = (
                    ytm_ref[k, pl.ds(ph * tp * nb + b, tp, stride=nb), :]).astype(y_ref.dtype)

    b_proj(0)
    for ph in range(nph):
        if ph + 1 < nph:
            b_proj(ph + 1)
        scan(ph)
        c_proj(ph)


def _s5(u, h0, are_b, aim_b, bblk, cblk, d_row, tt):
    b, t, w = u.shape
    assert b == SUBLANES and t % tt == 0 and tt % SUBLANES == 0
    pitch = tt + SUBLANES
    rows = tt * b
    const = lambda shape: pl.BlockSpec(shape, lambda j: (0,) * len(shape))
    return pl.pallas_call(
        functools.partial(_s5_kernel, tt=tt, pitch=pitch),
        grid=(t // tt,),
        in_specs=[pl.BlockSpec((b, tt, w), lambda j: (0, j, 0)),
                  const((b, STATE_LANES)), const((b, STATE_LANES)), const((b, STATE_LANES)),
                  const(bblk.shape), const(cblk.shape), const((1, w))],
        out_specs=[pl.BlockSpec((b, tt, w), lambda j: (0, j, 0)), const((b, STATE_LANES))],
        out_shape=[jax.ShapeDtypeStruct((b, t, w), BF16), jax.ShapeDtypeStruct((b, STATE_LANES), F32)],
        scratch_shapes=[pltpu.VMEM((w // LANES, b * pitch, LANES), F32),
                        pltpu.VMEM((rows, w), F32),
                        pltpu.VMEM((rows, STATE_LANES), F32),
                        pltpu.VMEM((w // LANES, rows, LANES), F32)],
        compiler_params=_cparams(("arbitrary",)),
        name="s5",
    )(u, h0, are_b, aim_b, bblk, cblk, d_row)


def _attn_kernel(*refs, bq, has_past):
    if has_past:
        (q_ref, kc_ref, vc_ref, cq_ref, ckc_ref, cec_ref, gq_ref, gk_ref, za_ref,
         kp_ref, vp_ref, ckp_ref, cep_ref, o_ref) = refs
    else:
        q_ref, kc_ref, vc_ref, cq_ref, ckc_ref, cec_ref, gq_ref, gk_ref, za_ref, o_ref = refs
    hp = pl.program_id(1)
    heads_per_step = q_ref.shape[2] // HEAD_DIM
    heads = range(heads_per_step)
    nq = q_ref.shape[1] // bq

    smax = (jnp.max(jnp.abs(gq_ref[...]), axis=-1, keepdims=True)
            * jnp.max(jnp.abs(gk_ref[...]), axis=-1, keepdims=True)
            * (HEAD_DIM * ATTN_SCALE * BF16_NORM_SLACK * BF16_NORM_SLACK))
    skip_below = -(SKIP_LOGIT_GAP + 2.0 * smax)
    smax2 = smax * LOG2E

    def load_cur(ref, hh, start, n):
        return ref[0, hh, pl.ds(start, n), :]

    def load_past(ref, hh, start, n):
        return ref[0, pl.ds(start, n), hh * HEAD_DIM:(hh + 1) * HEAD_DIM].astype(BF16)

    cur = (load_cur, kc_ref, vc_ref, ckc_ref)
    past = (load_past, kp_ref, vp_ref, ckp_ref) if has_past else None

    def q_block(qi, online):
        row0 = pl.multiple_of(qi * bq, bq)
        qrows = pl.ds(row0, bq)
        cq_all = cq_ref[0, qrows, :]
        head_lane = lax.broadcasted_iota(jnp.int32, cq_all.shape, 1)
        qs = [q_ref[0, qrows, hh * HEAD_DIM:(hh + 1) * HEAD_DIM] for hh in heads]

        def cq_of(hh, rows):
            h = hp * heads_per_step + hh
            return jnp.sum(jnp.where(head_lane[rows] == h, cq_all[rows], 0.0), axis=-1, keepdims=True)

        def first_live(ce_ref, n_visible):
            dead = None
            for hh in heads:
                cq_first = cq_of(hh, slice(0, 1))
                ce = ce_ref[0, hh]
                idx = lax.broadcasted_iota(jnp.int32, ce.shape, 1)
                d = jnp.logical_and(cq_first - ce < skip_below, idx < n_visible)
                dead = d if dead is None else jnp.logical_and(dead, d)
            return jnp.sum(dead.astype(jnp.int32))

        def finish(outs):
            o = jnp.concatenate(outs, axis=-1)
            o_ref[0, qrows, :] = (o * _silu(za_ref[0, qrows, :].astype(F32))).astype(BF16)

        def sweep(block, src, lo, hi, carry):
            odd = lo + jnp.bitwise_and(hi - lo, 1)
            carry = lax.fori_loop(lo, odd, lambda j, c: block(src, j, c), carry)

            def pair(i, c):
                j = odd + 2 * i
                return block(src, j + 1, block(src, j, c))
            return lax.fori_loop(0, lax.shift_right_logical(hi - odd, 1), pair, carry)

        def all_blocks(block, carry):
            if has_past:
                npb = ckp_ref.shape[2]
                carry = sweep(block, past, first_live(cep_ref, npb), npb, carry)
            return sweep(block, cur, first_live(cec_ref, qi), qi, carry)

        if not online:
            refs_row = [smax2 - cq_of(hh, slice(None)) * LOG2E for hh in heads]

            def piece(src, hh, j, off, sub, causal, carry_h):
                load, k_ref, v_ref, ck_ref = src
                lp, acc = carry_h
                bk = ck_ref.shape[4]
                start = pl.multiple_of(j * bk + off, sub)
                s = lax.dot_general(qs[hh], load(k_ref, hh, start, sub), (((1,), (1,)), ((), ())),
                                    preferred_element_type=F32)
                t = s - ck_ref[0, hh, j][:, off:off + sub] * LOG2E - refs_row[hh]
                if causal:
                    r = lax.broadcasted_iota(jnp.int32, t.shape, 0)
                    c = lax.broadcasted_iota(jnp.int32, t.shape, 1) + off
                    t = jnp.where(r >= c, t, NEG_BIG)
                p = jnp.exp2(t)
                if sub < LANES:
                    psum = jnp.concatenate([p, jnp.zeros((p.shape[0], LANES - sub), F32)], axis=-1)
                else:
                    psum = p[:, 0:LANES]
                    for c0 in range(LANES, sub, LANES):
                        psum = psum + p[:, c0:c0 + LANES]
                pv = jnp.dot(p.astype(BF16), load(v_ref, hh, start, sub), preferred_element_type=F32)
                return lp + psum, acc + pv

            def block(src, j, carry):
                bk = src[3].shape[4]
                sub = min(bk, ATTN_SUB)
                new = []
                for hh in heads:
                    c = carry[hh]
                    for off in range(0, bk, sub):
                        c = piece(src, hh, j, off, sub, False, c)
                    new.append(c)
                return tuple(new)

            carry = tuple((jnp.zeros((bq, LANES), F32), jnp.zeros((bq, HEAD_DIM), F32)) for _ in heads)
            carry = all_blocks(block, carry)
            outs = []
            for hh in heads:
                lp, acc = piece(cur, hh, qi, 0, bq, True, carry[hh])
                outs.append(acc / jnp.sum(lp, axis=-1, keepdims=True))
            finish(outs)
        else:
            def update(s, v, carry_h):
                m, l, acc = carry_h
                m_new = jnp.maximum(m, jnp.max(s, axis=-1, keepdims=True))
                alpha = jnp.exp2(m - m_new)
                p = jnp.exp2(s - m_new)
                l = alpha * l + jnp.sum(p, axis=-1, keepdims=True)
                acc = alpha * acc + jnp.dot(p.astype(BF16), v, preferred_element_type=F32)
                return m_new, l, acc

            def scores(src, hh, j):
                load, k_ref, _, ck_ref = src
                bk = ck_ref.shape[4]
                start = pl.multiple_of(j * bk, bk)
                s = lax.dot_general(qs[hh], load(k_ref, hh, start, bk), (((1,), (1,)), ((), ())),
                                    preferred_element_type=F32)
                return s - ck_ref[0, hh, j] * LOG2E, start, bk

            def block(src, j, carry):
                new = []
                for hh in heads:
                    s, start, bk = scores(src, hh, j)
                    new.append(update(s, src[0](src[2], hh, start, bk), carry[hh]))
                return tuple(new)

            carry = tuple((jnp.full((bq, 1), NEG_BIG, F32), jnp.zeros((bq, 1), F32),
                           jnp.zeros((bq, HEAD_DIM), F32)) for _ in heads)
            carry = all_blocks(block, carry)
            row = lax.broadcasted_iota(jnp.int32, (bq, bq), 0)
            col = lax.broadcasted_iota(jnp.int32, (bq, bq), 1)
            outs = []
            for hh in heads:
                s, start, bk = scores(cur, hh, qi)
                s = jnp.where(row >= col, s, NEG_BIG)
                _, l, acc = update(s, load_cur(vc_ref, hh, start, bk), carry[hh])
                outs.append(acc / l)
            finish(outs)
        return 0

    small_logits = jnp.max(smax2) * 2.0 <= BOUNDED_LOGIT_RANGE

    @pl.when(small_logits)
    def _():
        lax.fori_loop(0, nq, lambda qi, _: q_block(qi, False), 0)

    @pl.when(jnp.logical_not(small_logits))
    def _():
        lax.fori_loop(0, nq, lambda qi, _: q_block(qi, True), 0)


def _row_blocks(cum_rows, blk):
    b, h, t = cum_rows.shape
    rows = cum_rows.reshape(b, h, t // blk, 1, blk)
    return rows, rows[:, :, :, 0, blk - 1].reshape(b, h, 1, t // blk)


def _attn(q, k_cur, v_cur, cum, cum_rows, za, gq, gk, bq, past=None):
    b, tq, _ = q.shape
    hps = LANES // HEAD_DIM
    nq = tq // bq
    ckc, cec = _row_blocks(cum_rows, bq)
    qspec = pl.BlockSpec((1, tq, LANES), lambda i, h: (i, 0, h))
    kvspec = pl.BlockSpec((1, hps, tq, HEAD_DIM), lambda i, h: (i, h, 0, 0))
    rowspec = lambda n, w: pl.BlockSpec((1, hps, n, 1, w), lambda i, h: (i, h, 0, 0, 0))
    endspec = lambda n: pl.BlockSpec((1, hps, 1, n), lambda i, h: (i, h, 0, 0))
    gspec = pl.BlockSpec((1, ATTN_WIDTH), lambda i, h: (0, 0))
    in_specs = [qspec, kvspec, kvspec, pl.BlockSpec((1, tq, N_HEADS), lambda i, h: (i, 0, 0)),
                rowspec(nq, bq), endspec(nq), gspec, gspec, qspec]
    args = [q, k_cur, v_cur, cum, ckc, cec, gq, gk, za]
    if past is not None:
        k_past, v_past, cum_past_rows, bkp = past
        tp = k_past.shape[1]
        ckp, cep = _row_blocks(cum_past_rows, bkp)
        pastspec = pl.BlockSpec((1, tp, LANES), lambda i, h: (i, 0, h))
        in_specs += [pastspec, pastspec, rowspec(tp // bkp, bkp), endspec(tp // bkp)]
        args += [k_past, v_past, ckp, cep]
    return pl.pallas_call(
        functools.partial(_attn_kernel, bq=bq, has_past=past is not None),
        grid=(b, N_HEADS // hps),
        in_specs=in_specs,
        out_specs=qspec,
        out_shape=jax.ShapeDtypeStruct((b, tq, ATTN_WIDTH), BF16),
        compiler_params=_cparams(("arbitrary", "arbitrary")),
        name="attn",
    )(*args)


def _out_kernel(x_ref, mod_ref, ys_ref, zs_ref, ya_ref, wglu_ref, bglu_ref, wo_ref, o_ref):
    y = ys_ref[0].astype(F32)
    g = y * (0.5 * (1.0 + jnp.tanh(GELU_C * (y + 0.044715 * (y * y * y)))))
    glu = jnp.dot(g.astype(BF16), wglu_ref[...], preferred_element_type=F32) + bglu_ref[...]
    y1 = g * _sigmoid(glu) * _silu(zs_ref[0].astype(F32))
    mixed = (jnp.dot(y1.astype(BF16), wo_ref[0:SSM_WIDTH, :], preferred_element_type=F32)
             + jnp.dot(ya_ref[0], wo_ref[SSM_WIDTH:, :], preferred_element_type=F32))
    o_ref[0] = x_ref[0] + mod_ref[0, 2:3, :] * mixed


def _out(x, mod3, y_ssm, zs, y_att, w_glu, b_glu, w_out, tm):
    b, t, d = x.shape
    const = lambda shape: pl.BlockSpec(shape, lambda i, j: (0,) * len(shape))
    tokspec = lambda w: pl.BlockSpec((1, tm, w), lambda i, j: (i, j, 0))
    return pl.pallas_call(
        _out_kernel,
        grid=(b, t // tm),
        in_specs=[tokspec(d), pl.BlockSpec((1, 3, d), lambda i, j: (i, 0, 0)),
                  tokspec(SSM_WIDTH), tokspec(SSM_WIDTH), tokspec(ATTN_WIDTH),
                  const(w_glu.shape), const((1, SSM_WIDTH)), const(w_out.shape)],
        out_specs=tokspec(d),
        out_shape=jax.ShapeDtypeStruct((b, t, d), F32),
        compiler_params=_cparams(("arbitrary", "arbitrary")),
        name="out",
    )(x, mod3, y_ssm, zs, y_att, w_glu, b_glu, w_out)


def _tile(t, pref):
    return pref if t % pref == 0 else t


def _layer(x, mod3, h0, past, p):
    b, t, d = x.shape
    tm = _tile(t, 512)
    if past is None:
        cum_init = jnp.zeros((b, 1, LANES), F32)
        attn_past = None
    else:
        ck_past, cv_past, clogf = past
        tp = ck_past.shape[1]
        cum_past_rows = _cumsum_rows(jnp.transpose(clogf.astype(F32), (0, 2, 1)), _tile(tp, 512))
        cum_init = jnp.pad(cum_past_rows[:, :, tp - 1], ((0, 0), (0, LANES - N_HEADS))).reshape(b, 1, LANES)
        attn_past = (ck_past.reshape(b, tp, ATTN_WIDTH), cv_past.reshape(b, tp, ATTN_WIDTH),
                     cum_past_rows, _tile(tp, 512))

    u, zs, q, k_out, v_out, k_att, v_att, za, logf, cum, cum_rows = _inproj(
        x, mod3, p['norm_g'], p['w_main'], p['w_f'], p['b_f'], p['gq'], p['gk'], p['e_mat'], cum_init, tm)
    y_ssm, h_new = _s5(u, h0, p['are_b'], p['aim_b'], p['bblk'], p['cblk'], p['d_row'], _tile(t, 128))
    y_att = _attn(q, k_att, v_att, cum, cum_rows, za, p['gq'], p['gk'], _tile(t, 512), attn_past)
    y = _out(x, mod3, y_ssm, zs, y_att, p['w_glu'], p['b_glu'], p['w_out'], _tile(t, 2048))
    return y, k_out, v_out, logf, h_new


def kernel(x_prompt, x_sample, cache_k, cache_v, cache_logf, state_ssm_re, state_ssm_im, c_prompt, c_sample,
           w_ada, b_ada, norm_g, w_in, b_f, q_norm_g, k_norm_g, ssm_log_dt, ssm_a_re, ssm_a_im,
           ssm_b_re, ssm_b_im, ssm_c_re, ssm_c_im, ssm_d, w_glu, b_glu, w_out):
    depth = w_ada.shape[0]
    d = x_prompt.shape[-1]
    bp = x_prompt.shape[0]
    bs = x_sample.shape[0]
    xp, xs = x_prompt, x_sample
    outs_p, outs_s = [], []
    for l in range(depth):
        mod = _mod(jnp.concatenate([c_prompt, c_sample], axis=0), w_ada[l], b_ada[l])
        mod3 = mod.reshape(bp + bs, 3, d)
        abar_re, abar_im, bbar_re, bbar_im = _zoh(ssm_log_dt[l], ssm_a_re[l], ssm_a_im[l],
                                                  ssm_b_re[l], ssm_b_im[l])
        a_lanes_re = _state_layout(abar_re, abar_re)
        a_lanes_im = _state_layout(abar_im, abar_im)
        split = 2 * SSM_WIDTH + 4 * ATTN_WIDTH
        hd = jnp.arange(MXU_WIDTH) // HEAD_DIM
        p = dict(
            norm_g=norm_g[l].reshape(1, d),
            w_main=w_in[l][:, :split].astype(BF16),
            w_f=jnp.pad(w_in[l][:, split:], ((0, 0), (0, LANES - N_HEADS))).astype(BF16),
            b_f=jnp.pad(b_f[l], (0, LANES - N_HEADS)).reshape(1, LANES),
            gq=jnp.tile(q_norm_g[l], N_HEADS).reshape(1, ATTN_WIDTH),
            gk=jnp.tile(k_norm_g[l], N_HEADS).reshape(1, ATTN_WIDTH),
            e_mat=(hd[:, None] == hd[None, :]).astype(BF16) * (1.0 / HEAD_DIM),
            are_b=jnp.broadcast_to(a_lanes_re[None], (SUBLANES, STATE_LANES)),
            aim_b=jnp.broadcast_to(a_lanes_im[None], (SUBLANES, STATE_LANES)),
            bblk=_block_diag_b(bbar_re, bbar_im),
            cblk=_block_diag_c(ssm_c_re[l], ssm_c_im[l]),
            d_row=ssm_d[l].reshape(1, SSM_WIDTH),
            w_glu=w_glu[l].astype(BF16), b_glu=b_glu[l].reshape(1, SSM_WIDTH),
            w_out=w_out[l].astype(BF16),
        )
        h0_p = jnp.zeros((bp, STATE_LANES), F32)
        xp, k1, v1, f1, h1 = _layer(xp, mod3[:bp], h0_p, None, p)
        h0_s = _state_layout(state_ssm_re[l], state_ssm_im[l])
        xs, k2, v2, f2, h2 = _layer(xs, mod3[bp:], h0_s, (cache_k[l], cache_v[l], cache_logf[l]), p)
        outs_p.append((k1, v1, f1) + _state_unlayout(h1))
        outs_s.append((k2, v2, f2) + _state_unlayout(h2))

    def stack(items, i, shape_tail):
        arr = jnp.stack([it[i] for it in items])
        return arr.reshape(arr.shape[:3] + shape_tail) if shape_tail else arr

    hd_tail = (N_HEADS, HEAD_DIM)
    return (xp, xs,
            stack(outs_p, 0, hd_tail), stack(outs_p, 1, hd_tail), stack(outs_p, 2, ()),
            stack(outs_p, 3, ()), stack(outs_p, 4, ()),
            stack(outs_s, 0, hd_tail), stack(outs_s, 1, hd_tail), stack(outs_s, 2, ()),
            stack(outs_s, 3, ()), stack(outs_s, 4, ()))
```

```python
import functools

import numpy as np
import jax
import jax.numpy as jnp
from jax import lax
from jax.experimental import pallas as pl
from jax.experimental.pallas import tpu as pltpu

F32 = jnp.float32
BF16 = jnp.bfloat16

N_HEADS = 8
HEAD_DIM = 64
ATTN_WIDTH = N_HEADS * HEAD_DIM
SSM_GROUPS = 32
SSM_GROUP = 16
SSM_STATE = 64
SSM_WIDTH = SSM_GROUPS * SSM_GROUP
STATE_LANES = 2 * SSM_GROUPS * SSM_STATE
HALF_LANES = STATE_LANES // 2
HALF_CH = SSM_WIDTH // 2
ATTN_SCALE = HEAD_DIM ** -0.5
NORM_EPS = 1e-6
LANES = 128
SUBLANES = 8
MXU_WIDTH = 256
NEG_BIG = -1e30
LOG2E = float(np.log2(np.e))
GELU_C = float(np.float32(np.sqrt(2.0 / np.pi)))
VMEM_LIMIT = 58 * 1024 * 1024
SKIP_LOGIT_GAP = 105.0
BF16_NORM_SLACK = 1.02
S5_PHASES = 2
ATTN_SUB = 512
BOUNDED_LOGIT_RANGE = 96.0


def _cparams(sem):
    return pltpu.CompilerParams(dimension_semantics=sem, vmem_limit_bytes=VMEM_LIMIT)


def _sigmoid(x):
    return 1.0 / (1.0 + jnp.exp(-x))


def _silu(x):
    return x * _sigmoid(x)


def _split3(x):
    hi = x.astype(BF16)
    r1 = x - hi.astype(F32)
    mid = r1.astype(BF16)
    lo = (r1 - mid.astype(F32)).astype(BF16)
    return hi, mid, lo


def _tri(n):
    return jnp.tril(jnp.ones((n, n), F32)).astype(BF16)


def _mod_kernel(c_ref, w_ref, b_ref, o_ref):
    c = c_ref[...]
    o_ref[...] = jnp.dot(_silu(c), w_ref[...], preferred_element_type=F32) + b_ref[...]


def _mod(c, w_ada, b_ada):
    n, d = c.shape
    n_out = w_ada.shape[1]
    blk = 1024
    return pl.pallas_call(
        _mod_kernel,
        grid=(n_out // blk,),
        in_specs=[pl.BlockSpec((n, d), lambda j: (0, 0)),
                  pl.BlockSpec((d, blk), lambda j: (0, j)),
                  pl.BlockSpec((1, blk), lambda j: (0, j))],
        out_specs=pl.BlockSpec((n, blk), lambda j: (0, j)),
        out_shape=jax.ShapeDtypeStruct((n, n_out), F32),
        compiler_params=_cparams(("arbitrary",)),
        name="mod",
    )(c, w_ada, b_ada.reshape(1, n_out))


def _zoh_kernel(ldt_ref, are_ref, aim_ref, bre_ref, bim_ref,
                abre_ref, abim_ref, bbre_ref, bbim_ref):
    dt = jnp.exp(ldt_ref[...])
    a_re = are_ref[...]
    a_im = aim_ref[...]
    mag = jnp.exp(a_re * dt)
    ang = a_im * dt
    abar_re = mag * jnp.cos(ang)
    abar_im = mag * jnp.sin(ang)
    den = a_re * a_re + a_im * a_im
    n_re = abar_re - 1.0
    n_im = abar_im
    q_re = (n_re * a_re + n_im * a_im) / den
    q_im = (n_im * a_re - n_re * a_im) / den
    b_re = bre_ref[...]
    b_im = bim_ref[...]
    abre_ref[...] = abar_re
    abim_ref[...] = abar_im
    bbre_ref[...] = q_re * b_re - q_im * b_im
    bbim_ref[...] = q_re * b_im + q_im * b_re


def _zoh(log_dt, a_re, a_im, b_re, b_im):
    g, n, p = b_re.shape
    rows = g * n
    ldt = jnp.broadcast_to(log_dt[:, None], (g, n)).reshape(rows, 1)
    outs = pl.pallas_call(
        _zoh_kernel,
        out_shape=[jax.ShapeDtypeStruct((rows, 1), F32), jax.ShapeDtypeStruct((rows, 1), F32),
                   jax.ShapeDtypeStruct((rows, p), F32), jax.ShapeDtypeStruct((rows, p), F32)],
        name="zoh",
    )(ldt, a_re.reshape(rows, 1), a_im.reshape(rows, 1), b_re.reshape(rows, p), b_im.reshape(rows, p))
    abar_re, abar_im, bbar_re, bbar_im = outs
    return (abar_re.reshape(g, n), abar_im.reshape(g, n),
            bbar_re.reshape(g, n, p), bbar_im.reshape(g, n, p))


def _state_layout(x_re, x_im):
    lead = x_re.shape[:-2]
    hg = SSM_GROUPS // 2
    r = x_re.reshape(lead + (2, 1, hg * SSM_STATE))
    i = x_im.reshape(lead + (2, 1, hg * SSM_STATE))
    return jnp.concatenate([r, i], axis=-2).reshape(lead + (STATE_LANES,))


def _state_unlayout(h):
    lead = h.shape[:-1]
    hg = SSM_GROUPS // 2
    x = h.reshape(lead + (2, 2, hg, SSM_STATE))
    re = x[..., :, 0, :, :].reshape(lead + (SSM_GROUPS, SSM_STATE))
    im = x[..., :, 1, :, :].reshape(lead + (SSM_GROUPS, SSM_STATE))
    return re, im


def _block_diag_b(bbar_re, bbar_im):
    hg = SSM_GROUPS // 2
    rows, cols = hg * SSM_GROUP, hg * SSM_STATE
    own = (jnp.arange(rows) // SSM_GROUP)[:, None] == (jnp.arange(cols) // SSM_STATE)[None, :]
    out = []
    for hf in range(2):
        parts = []
        for bb in (bbar_re, bbar_im):
            b = jnp.transpose(bb[hf * hg:(hf + 1) * hg], (0, 2, 1)).reshape(rows, SSM_STATE)
            parts.append(jnp.where(own, jnp.tile(b, (1, hg)), 0.0))
        out.append(jnp.concatenate(parts, axis=1))
    return jnp.stack(out).astype(BF16)


def _block_diag_c(c_re, c_im):
    hg = SSM_GROUPS // 2
    rows, cols = hg * SSM_STATE, hg * SSM_GROUP
    own = (jnp.arange(rows) // SSM_STATE)[:, None] == (jnp.arange(cols) // SSM_GROUP)[None, :]
    out = []
    for hf in range(2):
        parts = []
        for cc in (c_re, -c_im):
            c = jnp.transpose(cc[hf * hg:(hf + 1) * hg], (0, 2, 1)).reshape(rows, SSM_GROUP)
            parts.append(jnp.where(own, jnp.tile(c, (1, hg)), 0.0))
        out.append(jnp.concatenate(parts, axis=0))
    return jnp.stack(out).astype(BF16)


def _inproj_kernel(x_ref, mod_ref, g_ref, wm_ref, wf_ref, bf_ref, gq_ref, gk_ref, e_ref, tri_ref, cinit_ref,
                   u_ref, zs_ref, q_ref, kout_ref, vout_ref, katt_ref, vatt_ref, za_ref, cum_ref, logf_ref,
                   *rest, rows):
    if rows:
        cum_rows_ref, carry_ref = rest
    else:
        carry_ref, = rest

    @pl.when(pl.program_id(1) == 0)
    def _():
        carry_ref[...] = cinit_ref[0]

    x = x_ref[0]
    ms = jnp.mean(x * x, axis=-1, keepdims=True)
    xn = x * lax.rsqrt(ms + NORM_EPS) * g_ref[...]
    shift = mod_ref[0, 0:1, :]
    scale = mod_ref[0, 1:2, :]
    hb = (xn * (1.0 + scale) + shift).astype(BF16)

    def sec(i):
        return jnp.dot(hb, wm_ref[:, i * 512:(i + 1) * 512], preferred_element_type=F32)

    def head_rmsnorm(y, g):
        y2 = (y * y).astype(BF16)
        ew = e_ref.shape[0]
        msq = jnp.concatenate([jnp.dot(y2[:, c:c + ew], e_ref[...], preferred_element_type=F32)
                               for c in range(0, y.shape[1], ew)], axis=-1)
        return y * lax.rsqrt(msq + NORM_EPS) * g

    u_ref[0] = sec(0)
    zs_ref[0] = sec(1).astype(BF16)
    q = head_rmsnorm(sec(2), gq_ref[...])
    q_ref[0] = (q * (ATTN_SCALE * LOG2E)).astype(BF16)
    k = head_rmsnorm(sec(3), gk_ref[...])
    kout_ref[0] = k
    v = sec(4)
    vout_ref[0] = v
    kb = k.astype(BF16)
    vb = v.astype(BF16)
    for h in range(N_HEADS):
        katt_ref[0, h] = kb[:, h * HEAD_DIM:(h + 1) * HEAD_DIM]
        vatt_ref[0, h] = vb[:, h * HEAD_DIM:(h + 1) * HEAD_DIM]
    za_ref[0] = sec(5).astype(BF16)
    f = jnp.dot(hb, wf_ref[...], preferred_element_type=F32) + bf_ref[...]
    logf = jnp.minimum(f, 0.0) - jnp.log1p(jnp.exp(-jnp.abs(f)))
    if rows:
        logf_ref[0] = logf.T[:N_HEADS, :]
    else:
        logf_ref[0] = logf[:, :N_HEADS]
    lane = lax.broadcasted_iota(jnp.int32, logf.shape, 1)
    hi, mid, lo = (part.astype(F32) for part in _split3(jnp.where(lane < N_HEADS, logf, 0.0)))
    packed = hi + pltpu.roll(mid, N_HEADS, 1) + pltpu.roll(lo, 2 * N_HEADS, 1)
    th = tri_ref.shape[0]
    packed = packed.astype(BF16)
    chunks, above = [], jnp.zeros((1, LANES), F32)
    for r0 in range(0, packed.shape[0], th):
        chunks.append(jnp.dot(tri_ref[...], packed[r0:r0 + th], preferred_element_type=F32) + above)
        above = above + jnp.sum(packed[r0:r0 + th].astype(F32), axis=0, keepdims=True)
    r = jnp.concatenate(chunks, axis=0)
    c = carry_ref[...] + r + pltpu.roll(r, LANES - N_HEADS, 1) + pltpu.roll(r, LANES - 2 * N_HEADS, 1)
    cum_ref[0] = c[:, :N_HEADS]
    if rows:
        cum_rows_ref[0] = c.T[:N_HEADS, :]
    tm = c.shape[0]
    carry_ref[...] = c[tm - 1:tm, :]


def _inproj(x, mod3, norm_g, w_main, w_f, b_f, gq, gk, e_mat, cum_init, tm):
    b, t, d = x.shape
    nt = t // tm
    tok = lambda w, dt: jax.ShapeDtypeStruct((b, t, w), dt)
    heads = jax.ShapeDtypeStruct((b, N_HEADS, t, HEAD_DIM), BF16)
    const = lambda shape: pl.BlockSpec(shape, lambda i, j: (0,) * len(shape))
    tokspec = lambda w: pl.BlockSpec((1, tm, w), lambda i, j: (i, j, 0))
    headspec = pl.BlockSpec((1, N_HEADS, tm, HEAD_DIM), lambda i, j: (i, 0, j, 0))
    rows = tm % LANES == 0
    th = MXU_WIDTH if tm % MXU_WIDTH == 0 else tm
    rowspec = pl.BlockSpec((1, N_HEADS, tm), lambda i, j: (i, 0, j))
    rowshape = jax.ShapeDtypeStruct((b, N_HEADS, t), F32)
    out_specs = [tokspec(SSM_WIDTH), tokspec(SSM_WIDTH), tokspec(ATTN_WIDTH), tokspec(ATTN_WIDTH),
                 tokspec(ATTN_WIDTH), headspec, headspec, tokspec(ATTN_WIDTH), tokspec(N_HEADS)]
    out_shape = [tok(SSM_WIDTH, F32), tok(SSM_WIDTH, BF16), tok(ATTN_WIDTH, BF16), tok(ATTN_WIDTH, F32),
                 tok(ATTN_WIDTH, F32), heads, heads, tok(ATTN_WIDTH, BF16), tok(N_HEADS, F32)]
    out_specs += [rowspec, rowspec] if rows else [tokspec(N_HEADS)]
    out_shape += [rowshape, rowshape] if rows else [tok(N_HEADS, F32)]
    outs = pl.pallas_call(
        functools.partial(_inproj_kernel, rows=rows),
        grid=(b, nt),
        in_specs=[tokspec(d),
                  pl.BlockSpec((1, 3, d), lambda i, j: (i, 0, 0)),
                  const((1, d)), const(w_main.shape), const(w_f.shape), const((1, LANES)),
                  const((1, ATTN_WIDTH)), const((1, ATTN_WIDTH)), const(e_mat.shape), const((th, th)),
                  pl.BlockSpec((1, 1, LANES), lambda i, j: (i, 0, 0))],
        out_specs=out_specs,
        out_shape=out_shape,
        scratch_shapes=[pltpu.VMEM((1, LANES), F32)],
        compiler_params=_cparams(("arbitrary", "arbitrary")),
        name="inproj",
    )(x, mod3, norm_g, w_main, w_f, b_f, gq, gk, e_mat, _tri(th), cum_init)
    main, cum = outs[:8], outs[8]
    if rows:
        logf_rows, cum_rows = outs[9:]
        logf = jnp.transpose(logf_rows, (0, 2, 1))
    else:
        logf = outs[9]
        cum_rows = jnp.transpose(cum, (0, 2, 1))
    return (*main, logf, cum, cum_rows)


def _cumsum_rows_kernel(x_ref, triu_ref, o_ref):
    tc = triu_ref.shape[0]
    total = jnp.zeros((x_ref.shape[1], 1), F32)
    for c0 in range(0, x_ref.shape[2], tc):
        c = total
        for part in _split3(x_ref[0, :, c0:c0 + tc]):
            c = c + jnp.dot(part, triu_ref[...], preferred_element_type=F32)
        o_ref[0, :, c0:c0 + tc] = c
        total = c[:, tc - 1:tc]


def _cumsum_rows(x, tc):
    b, h, t = x.shape
    return pl.pallas_call(
        _cumsum_rows_kernel,
        grid=(b,),
        in_specs=[pl.BlockSpec((1, h, t), lambda i: (i, 0, 0)),
                  pl.BlockSpec((tc, tc), lambda i: (0, 0))],
        out_specs=pl.BlockSpec((1, h, t), lambda i: (i, 0, 0)),
        out_shape=jax.ShapeDtypeStruct((b, h, t), F32),
        compiler_params=_cparams(("arbitrary",)),
        name="cumsum",
    )(x, jnp.triu(jnp.ones((tc, tc), F32)).astype(BF16))


def _s5_kernel(u_ref, h0_ref, are_ref, aim_ref, bblk_ref, cblk_ref, d_ref,
               y_ref, hout_ref,
               ut_ref, utm_ref, bu_ref, ytm_ref, *, tt, pitch):
    nb = u_ref.shape[0]
    nslab = SSM_WIDTH // LANES

    @pl.when(pl.program_id(0) == 0)
    def _():
        hout_ref[...] = h0_ref[...]

    for b in range(nb):
        for k in range(nslab):
            ut_ref[k, b * pitch:b * pitch + tt, :] = u_ref[b, :, k * LANES:(k + 1) * LANES]

    nph = S5_PHASES if tt % (S5_PHASES * SUBLANES) == 0 else 1
    tp = tt // nph
    chunk = 4 * LANES
    quarter = HALF_LANES // 2
    chunks = [(hf * HALF_LANES + c * chunk, hf * HALF_LANES + c * chunk + quarter)
              for hf in range(2) for c in range(quarter // chunk)]

    def b_proj(ph):
        for t in range(ph * tp, (ph + 1) * tp):
            for k in range(nslab):
                utm_ref[t * nb:(t + 1) * nb, k * LANES:(k + 1) * LANES] = ut_ref[k, pl.ds(t, nb, stride=pitch), :]
        rows = slice(ph * tp * nb, (ph + 1) * tp * nb)
        for hf in range(2):
            uh = utm_ref[rows, hf * HALF_CH:(hf + 1) * HALF_CH].astype(BF16)
            bu_ref[rows, hf * HALF_LANES:(hf + 1) * HALF_LANES] = jnp.dot(
                uh, bblk_ref[hf], preferred_element_type=F32)

    def scan(ph):
        state = [(hout_ref[:, lo_re:lo_re + chunk], hout_ref[:, lo_im:lo_im + chunk]) for lo_re, lo_im in chunks]
        for t in range(ph * tp, (ph + 1) * tp):
            r = slice(t * nb, (t + 1) * nb)
            for ci, (lo_re, lo_im) in enumerate(chunks):
                xr, xi = state[ci]
                ar = are_ref[:, lo_re:lo_re + chunk]
                ai = aim_ref[:, lo_re:lo_re + chunk]
                nr = ar * xr - ai * xi + bu_ref[r, lo_re:lo_re + chunk]
                ni = ar * xi + ai * xr + bu_ref[r, lo_im:lo_im + chunk]
                bu_ref[r, lo_re:lo_re + chunk] = nr
                bu_ref[r, lo_im:lo_im + chunk] = ni
                state[ci] = (nr, ni)
        for (lo_re, lo_im), (xr, xi) in zip(chunks, state):
            hout_ref[:, lo_re:lo_re + chunk] = xr
            hout_ref[:, lo_im:lo_im + chunk] = xi

    def c_proj(ph):
        rows = slice(ph * tp * nb, (ph + 1) * tp * nb)
        for hf in range(2):
            xh = bu_ref[rows, hf * HALF_LANES:(hf + 1) * HALF_LANES].astype(BF16)
            yh = jnp.dot(xh, cblk_ref[hf], preferred_element_type=F32)
            yh = yh + d_ref[:, hf * HALF_CH:(hf + 1) * HALF_CH] * utm_ref[rows, hf * HALF_CH:(hf + 1) * HALF_CH]
            for kk in range(HALF_CH // LANES):
                ytm_ref[hf * (HALF_CH // LANES) + kk, rows, :] = yh[:, kk * LANES:(kk + 1) * LANES]
        for b in range(nb):
            for k in range(nslab):
                y_ref[b, ph * tp:(ph + 1) * tp, k * LANES:(k + 1) * LANES] = (
                    ytm_ref[k, pl.ds(ph * tp * nb + b, tp, stride=nb), :]).astype(y_ref.dtype)

    b_proj(0)
    for ph in range(nph):
        if ph + 1 < nph:
            b_proj(ph + 1)
        scan(ph)
        c_proj(ph)


def _s5(u, h0, are_b, aim_b, bblk, cblk, d_row, tt):
    b, t, w = u.shape
    assert b == SUBLANES and t % tt == 0 and tt % SUBLANES == 0
    pitch = tt + SUBLANES
    rows = tt * b
    const = lambda shape: pl.BlockSpec(shape, lambda j: (0,) * len(shape))
    return pl.pallas_call(
        functools.partial(_s5_kernel, tt=tt, pitch=pitch),
        grid=(t // tt,),
        in_specs=[pl.BlockSpec((b, tt, w), lambda j: (0, j, 0)),
                  const((b, STATE_LANES)), const((b, STATE_LANES)), const((b, STATE_LANES)),
                  const(bblk.shape), const(cblk.shape), const((1, w))],
        out_specs=[pl.BlockSpec((b, tt, w), lambda j: (0, j, 0)), const((b, STATE_LANES))],
        out_shape=[jax.ShapeDtypeStruct((b, t, w), BF16), jax.ShapeDtypeStruct((b, STATE_LANES), F32)],
        scratch_shapes=[pltpu.VMEM((w // LANES, b * pitch, LANES), F32),
                        pltpu.VMEM((rows, w), F32),
                        pltpu.VMEM((rows, STATE_LANES), F32),
                        pltpu.VMEM((w // LANES, rows, LANES), F32)],
        compiler_params=_cparams(("arbitrary",)),
        name="s5",
    )(u, h0, are_b, aim_b, bblk, cblk, d_row)


def _attn_kernel(*refs, bq, has_past):
    if has_past:
        (q_ref, kc_ref, vc_ref, cq_ref, ckc_ref, cec_ref, gq_ref, gk_ref, za_ref,
         kp_ref, vp_ref, ckp_ref, cep_ref, o_ref) = refs
    else:
        q_ref, kc_ref, vc_ref, cq_ref, ckc_ref, cec_ref, gq_ref, gk_ref, za_ref, o_ref = refs
    hp = pl.program_id(1)
    heads_per_step = q_ref.shape[2] // HEAD_DIM
    heads = range(heads_per_step)
    nq = q_ref.shape[1] // bq

    smax = (jnp.max(jnp.abs(gq_ref[...]), axis=-1, keepdims=True)
            * jnp.max(jnp.abs(gk_ref[...]), axis=-1, keepdims=True)
            * (HEAD_DIM * ATTN_SCALE * BF16_NORM_SLACK * BF16_NORM_SLACK))
    skip_below = -(SKIP_LOGIT_GAP + 2.0 * smax)
    smax2 = smax * LOG2E

    def load_cur(ref, hh, start, n):
        return ref[0, hh, pl.ds(start, n), :]

    def load_past(ref, hh, start, n):
        return ref[0, pl.ds(start, n), hh * HEAD_DIM:(hh + 1) * HEAD_DIM]

    cur = (load_cur, kc_ref, vc_ref, ckc_ref)
    past = (load_past, kp_ref, vp_ref, ckp_ref) if has_past else None

    def q_block(qi, online):
        row0 = pl.multiple_of(qi * bq, bq)
        qrows = pl.ds(row0, bq)
        cq_all = cq_ref[0, qrows, :]
        head_lane = lax.broadcasted_iota(jnp.int32, cq_all.shape, 1)
        qs = [q_ref[0, qrows, hh * HEAD_DIM:(hh + 1) * HEAD_DIM] for hh in heads]

        def cq_of(hh, rows):
            h = hp * heads_per_step + hh
            return jnp.sum(jnp.where(head_lane[rows] == h, cq_all[rows], 0.0), axis=-1, keepdims=True)

        def first_live(ce_ref, n_visible):
            dead = None
            for hh in heads:
                cq_first = cq_of(hh, slice(0, 1))
                ce = ce_ref[0, hh]
                idx = lax.broadcasted_iota(jnp.int32, ce.shape, 1)
                d = jnp.logical_and(cq_first - ce < skip_below, idx < n_visible)
                dead = d if dead is None else jnp.logical_and(dead, d)
            return jnp.sum(dead.astype(jnp.int32))

        def finish(outs):
            o = jnp.concatenate(outs, axis=-1)
            o_ref[0, qrows, :] = (o * _silu(za_ref[0, qrows, :].astype(F32))).astype(BF16)

        def sweep(block, src, lo, hi, carry):
            odd = lo + jnp.bitwise_and(hi - lo, 1)
            carry = lax.fori_loop(lo, odd, lambda j, c: block(src, j, c), carry)

            def pair(i, c):
                j = odd + 2 * i
                return block(src, j + 1, block(src, j, c))
            return lax.fori_loop(0, lax.shift_right_logical(hi - odd, 1), pair, carry)

        def all_blocks(block, carry):
            if has_past:
                npb = ckp_ref.shape[2]
                carry = sweep(block, past, first_live(cep_ref, npb), npb, carry)
            return sweep(block, cur, first_live(cec_ref, qi), qi, carry)

        if not online:
            refs_row = [smax2 - cq_of(hh, slice(None)) * LOG2E for hh in heads]

            def piece(src, hh, j, off, sub, causal, carry_h):
                load, k_ref, v_ref, ck_ref = src
                lp, acc = carry_h
                bk = ck_ref.shape[4]
                start = pl.multiple_of(j * bk + off, sub)
                s = lax.dot_general(qs[hh], load(k_ref, hh, start, sub), (((1,), (1,)), ((), ())),
                                    preferred_element_type=F32)
                t = s - ck_ref[0, hh, j][:, off:off + sub] * LOG2E - refs_row[hh]
                if causal:
                    r = lax.broadcasted_iota(jnp.int32, t.shape, 0)
                    c = lax.broadcasted_iota(jnp.int32, t.shape, 1) + off
                    t = jnp.where(r >= c, t, NEG_BIG)
                p = jnp.exp2(t)
                if sub < LANES:
                    psum = jnp.concatenate([p, jnp.zeros((p.shape[0], LANES - sub), F32)], axis=-1)
                else:
                    psum = p[:, 0:LANES]
                    for c0 in range(LANES, sub, LANES):
                        psum = psum + p[:, c0:c0 + LANES]
                pv = jnp.dot(p.astype(BF16), load(v_ref, hh, start, sub), preferred_element_type=F32)
                return lp + psum, acc + pv

            def block(src, j, carry):
                bk = src[3].shape[4]
                sub = min(bk, ATTN_SUB)
                new = []
                for hh in heads:
                    c = carry[hh]
                    for off in range(0, bk, sub):
                        c = piece(src, hh, j, off, sub, False, c)
                    new.append(c)
                return tuple(new)

            carry = tuple((jnp.zeros((bq, LANES), F32), jnp.zeros((bq, HEAD_DIM), F32)) for _ in heads)
            carry = all_blocks(block, carry)
            outs = []
            for hh in heads:
                lp, acc = piece(cur, hh, qi, 0, bq, True, carry[hh])
                outs.append(acc / jnp.sum(lp, axis=-1, keepdims=True))
            finish(outs)
        else:
            def update(s, v, carry_h):
                m, l, acc = carry_h
                m_new = jnp.maximum(m, jnp.max(s, axis=-1, keepdims=True))
                alpha = jnp.exp2(m - m_new)
                p = jnp.exp2(s - m_new)
                l = alpha * l + jnp.sum(p, axis=-1, keepdims=True)
                acc = alpha * acc + jnp.dot(p.astype(BF16), v, preferred_element_type=F32)
                return m_new, l, acc

            def scores(src, hh, j):
                load, k_ref, _, ck_ref = src
                bk = ck_ref.shape[4]
                start = pl.multiple_of(j * bk, bk)
                s = lax.dot_general(qs[hh], load(k_ref, hh, start, bk), (((1,), (1,)), ((), ())),
                                    preferred_element_type=F32)
                return s - ck_ref[0, hh, j] * LOG2E, start, bk

            def block(src, j, carry):
                new = []
                for hh in heads:
                    s, start, bk = scores(src, hh, j)
                    new.append(update(s, src[0](src[2], hh, start, bk), carry[hh]))
                return tuple(new)

            carry = tuple((jnp.full((bq, 1), NEG_BIG, F32), jnp.zeros((bq, 1), F32),
                           jnp.zeros((bq, HEAD_DIM), F32)) for _ in heads)
            carry = all_blocks(block, carry)
            row = lax.broadcasted_iota(jnp.int32, (bq, bq), 0)
            col = lax.broadcasted_iota(jnp.int32, (bq, bq), 1)
            outs = []
            for hh in heads:
                s, start, bk = scores(cur, hh, qi)
                s = jnp.where(row >= col, s, NEG_BIG)
                _, l, acc = update(s, load_cur(vc_ref, hh, start, bk), carry[hh])
                outs.append(acc / l)
            finish(outs)
        return 0

    small_logits = jnp.max(smax2) * 2.0 <= BOUNDED_LOGIT_RANGE

    @pl.when(small_logits)
    def _():
        lax.fori_loop(0, nq, lambda qi, _: q_block(qi, False), 0)

    @pl.when(jnp.logical_not(small_logits))
    def _():
        lax.fori_loop(0, nq, lambda qi, _: q_block(qi, True), 0)


def _row_blocks(cum_rows, blk):
    b, h, t = cum_rows.shape
    rows = cum_rows.reshape(b, h, t // blk, 1, blk)
    return rows, rows[:, :, :, 0, blk - 1].reshape(b, h, 1, t // blk)


def _attn(q, k_cur, v_cur, cum, cum_rows, za, gq, gk, bq, past=None):
    b, tq, _ = q.shape
    hps = LANES // HEAD_DIM
    nq = tq // bq
    ckc, cec = _row_blocks(cum_rows, bq)
    qspec = pl.BlockSpec((1, tq, LANES), lambda i, h: (i, 0, h))
    kvspec = pl.BlockSpec((1, hps, tq, HEAD_DIM), lambda i, h: (i, h, 0, 0))
    rowspec = lambda n, w: pl.BlockSpec((1, hps, n, 1, w), lambda i, h: (i, h, 0, 0, 0))
    endspec = lambda n: pl.BlockSpec((1, hps, 1, n), lambda i, h: (i, h, 0, 0))
    gspec = pl.BlockSpec((1, ATTN_WIDTH), lambda i, h: (0, 0))
    in_specs = [qspec, kvspec, kvspec, pl.BlockSpec((1, tq, N_HEADS), lambda i, h: (i, 0, 0)),
                rowspec(nq, bq), endspec(nq), gspec, gspec, qspec]
    args = [q, k_cur, v_cur, cum, ckc, cec, gq, gk, za]
    if past is not None:
        k_past, v_past, cum_past_rows, bkp = past
        tp = k_past.shape[1]
        ckp, cep = _row_blocks(cum_past_rows, bkp)
        pastspec = pl.BlockSpec((1, tp, LANES), lambda i, h: (i, 0, h))
        in_specs += [pastspec, pastspec, rowspec(tp // bkp, bkp), endspec(tp // bkp)]
        args += [k_past, v_past, ckp, cep]
    return pl.pallas_call(
        functools.partial(_attn_kernel, bq=bq, has_past=past is not None),
        grid=(b, N_HEADS // hps),
        in_specs=in_specs,
        out_specs=qspec,
        out_shape=jax.ShapeDtypeStruct((b, tq, ATTN_WIDTH), BF16),
        compiler_params=_cparams(("arbitrary", "arbitrary")),
        name="attn",
    )(*args)


def _out_kernel(x_ref, mod_ref, ys_ref, zs_ref, ya_ref, wglu_ref, bglu_ref, wo_ref, o_ref):
    y = ys_ref[0].astype(F32)
    g = y * (0.5 * (1.0 + jnp.tanh(GELU_C * (y + 0.044715 * (y * y * y)))))
    glu = jnp.dot(g.astype(BF16), wglu_ref[...], preferred_element_type=F32) + bglu_ref[...]
    y1 = g * _sigmoid(glu) * _silu(zs_ref[0].astype(F32))
    mixed = (jnp.dot(y1.astype(BF16), wo_ref[0:SSM_WIDTH, :], preferred_element_type=F32)
             + jnp.dot(ya_ref[0], wo_ref[SSM_WIDTH:, :], preferred_element_type=F32))
    o_ref[0] = x_ref[0] + mod_ref[0, 2:3, :] * mixed


def _out(x, mod3, y_ssm, zs, y_att, w_glu, b_glu, w_out, tm):
    b, t, d = x.shape
    const = lambda shape: pl.BlockSpec(shape, lambda i, j: (0,) * len(shape))
    tokspec = lambda w: pl.BlockSpec((1, tm, w), lambda i, j: (i, j, 0))
    return pl.pallas_call(
        _out_kernel,
        grid=(b, t // tm),
        in_specs=[tokspec(d), pl.BlockSpec((1, 3, d), lambda i, j: (i, 0, 0)),
                  tokspec(SSM_WIDTH), tokspec(SSM_WIDTH), tokspec(ATTN_WIDTH),
                  const(w_glu.shape), const((1, SSM_WIDTH)), const(w_out.shape)],
        out_specs=tokspec(d),
        out_shape=jax.ShapeDtypeStruct((b, t, d), F32),
        compiler_params=_cparams(("arbitrary", "arbitrary")),
        name="out",
    )(x, mod3, y_ssm, zs, y_att, w_glu, b_glu, w_out)


def _tile(t, pref):
    return pref if t % pref == 0 else t


def _layer(x, mod3, h0, past, p):
    b, t, d = x.shape
    tm = _tile(t, 512)
    if past is None:
        cum_init = jnp.zeros((b, 1, LANES), F32)
        attn_past = None
    else:
        ck_past, cv_past, clogf = past
        tp = ck_past.shape[1]
        cum_past_rows = _cumsum_rows(jnp.transpose(clogf.astype(F32), (0, 2, 1)), _tile(tp, 512))
        cum_init = jnp.pad(cum_past_rows[:, :, tp - 1], ((0, 0), (0, LANES - N_HEADS))).reshape(b, 1, LANES)
        attn_past = (ck_past.reshape(b, tp, ATTN_WIDTH).astype(BF16), cv_past.reshape(b, tp, ATTN_WIDTH).astype(BF16),
                     cum_past_rows, _tile(tp, 512))

    u, zs, q, k_out, v_out, k_att, v_att, za, logf, cum, cum_rows = _inproj(
        x, mod3, p['norm_g'], p['w_main'], p['w_f'], p['b_f'], p['gq'], p['gk'], p['e_mat'], cum_init, tm)
    y_ssm, h_new = _s5(u, h0, p['are_b'], p['aim_b'], p['bblk'], p['cblk'], p['d_row'], _tile(t, 128))
    y_att = _attn(q, k_att, v_att, cum, cum_rows, za, p['gq'], p['gk'], _tile(t, 512), attn_past)
    y = _out(x, mod3, y_ssm, zs, y_att, p['w_glu'], p['b_glu'], p['w_out'], _tile(t, 2048))
    return y, k_out, v_out, logf, h_new


def kernel(x_prompt, x_sample, cache_k, cache_v, cache_logf, state_ssm_re, state_ssm_im, c_prompt, c_sample,
           w_ada, b_ada, norm_g, w_in, b_f, q_norm_g, k_norm_g, ssm_log_dt, ssm_a_re, ssm_a_im,
           ssm_b_re, ssm_b_im, ssm_c_re, ssm_c_im, ssm_d, w_glu, b_glu, w_out):
    depth = w_ada.shape[0]
    d = x_prompt.shape[-1]
    bp = x_prompt.shape[0]
    bs = x_sample.shape[0]
    xp, xs = x_prompt, x_sample
    outs_p, outs_s = [], []
    for l in range(depth):
        mod = _mod(jnp.concatenate([c_prompt, c_sample], axis=0), w_ada[l], b_ada[l])
        mod3 = mod.reshape(bp + bs, 3, d)
        abar_re, abar_im, bbar_re, bbar_im = _zoh(ssm_log_dt[l], ssm_a_re[l], ssm_a_im[l],
                                                  ssm_b_re[l], ssm_b_im[l])
        a_lanes_re = _state_layout(abar_re, abar_re)
        a_lanes_im = _state_layout(abar_im, abar_im)
        split = 2 * SSM_WIDTH + 4 * ATTN_WIDTH
        hd = jnp.arange(MXU_WIDTH) // HEAD_DIM
        p = dict(
            norm_g=norm_g[l].reshape(1, d),
            w_main=w_in[l][:, :split].astype(BF16),
            w_f=jnp.pad(w_in[l][:, split:], ((0, 0), (0, LANES - N_HEADS))).astype(BF16),
            b_f=jnp.pad(b_f[l], (0, LANES - N_HEADS)).reshape(1, LANES),
            gq=jnp.tile(q_norm_g[l], N_HEADS).reshape(1, ATTN_WIDTH),
            gk=jnp.tile(k_norm_g[l], N_HEADS).reshape(1, ATTN_WIDTH),
            e_mat=(hd[:, None] == hd[None, :]).astype(BF16) * (1.0 / HEAD_DIM),
            are_b=jnp.broadcast_to(a_lanes_re[None], (SUBLANES, STATE_LANES)),
            aim_b=jnp.broadcast_to(a_lanes_im[None], (SUBLANES, STATE_LANES)),
            bblk=_block_diag_b(bbar_re, bbar_im),
            cblk=_block_diag_c(ssm_c_re[l], ssm_c_im[l]),
            d_row=ssm_d[l].reshape(1, SSM_WIDTH),
            w_glu=w_glu[l].astype(BF16), b_glu=b_glu[l].reshape(1, SSM_WIDTH),
            w_out=w_out[l].astype(BF16),
        )
        h0_p = jnp.zeros((bp, STATE_LANES), F32)
        xp, k1, v1, f1, h1 = _layer(xp, mod3[:bp], h0_p, None, p)
        h0_s = _state_layout(state_ssm_re[l], state_ssm_im[l])
        xs, k2, v2, f2, h2 = _layer(xs, mod3[bp:], h0_s, (cache_k[l], cache_v[l], cache_logf[l]), p)
        outs_p.append((k1, v1, f1) + _state_unlayout(h1))
        outs_s.append((k2, v2, f2) + _state_unlayout(h2))

    def stack(items, i, shape_tail):
        arr = jnp.stack([it[i] for it in items])
        return arr.reshape(arr.shape[:3] + shape_tail) if shape_tail else arr

    hd_tail = (N_HEADS, HEAD_DIM)
    return (xp, xs,
            stack(outs_p, 0, hd_tail), stack(outs_p, 1, hd_tail), stack(outs_p, 2, ()),
            stack(outs_p, 3, ()), stack(outs_p, 4, ()),
            stack(outs_s, 0, hd_tail), stack(outs_s, 1, hd_tail), stack(outs_s, 2, ()),
            stack(outs_s, 3, ()), stack(outs_s, 4, ()))
```

```python
import functools

import numpy as np
import jax
import jax.numpy as jnp
from jax import lax
from jax.experimental import pallas as pl
from jax.experimental.pallas import tpu as pltpu

F32 = jnp.float32
BF16 = jnp.bfloat16

N_HEADS = 8
HEAD_DIM = 64
ATTN_WIDTH = N_HEADS * HEAD_DIM
SSM_GROUPS = 32
SSM_GROUP = 16
SSM_STATE = 64
SSM_WIDTH = SSM_GROUPS * SSM_GROUP
STATE_LANES = 2 * SSM_GROUPS * SSM_STATE
HALF_LANES = STATE_LANES // 2
HALF_CH = SSM_WIDTH // 2
ATTN_SCALE = HEAD_DIM ** -0.5
NORM_EPS = 1e-6
LANES = 128
SUBLANES = 8
MXU_WIDTH = 256
NEG_BIG = -1e30
LOG2E = float(np.log2(np.e))
GELU_C = float(np.float32(np.sqrt(2.0 / np.pi)))
VMEM_LIMIT = 58 * 1024 * 1024
SKIP_LOGIT_GAP = 105.0
BF16_NORM_SLACK = 1.02
S5_PHASES = 2
ATTN_SUB = 512
BOUNDED_LOGIT_RANGE = 96.0


def _cparams(sem):
    return pltpu.CompilerParams(dimension_semantics=sem, vmem_limit_bytes=VMEM_LIMIT)


def _sigmoid(x):
    return 1.0 / (1.0 + jnp.exp(-x))


def _silu(x):
    return x * _sigmoid(x)


def _split3(x):
    hi = x.astype(BF16)
    r1 = x - hi.astype(F32)
    mid = r1.astype(BF16)
    lo = (r1 - mid.astype(F32)).astype(BF16)
    return hi, mid, lo


def _tri(n):
    return jnp.tril(jnp.ones((n, n), F32)).astype(BF16)


def _mod_kernel(c_ref, w_ref, b_ref, o_ref):
    c = c_ref[...]
    o_ref[...] = jnp.dot(_silu(c), w_ref[...], preferred_element_type=F32) + b_ref[...]


def _mod(c, w_ada, b_ada):
    n, d = c.shape
    n_out = w_ada.shape[1]
    blk = 1024
    return pl.pallas_call(
        _mod_kernel,
        grid=(n_out // blk,),
        in_specs=[pl.BlockSpec((n, d), lambda j: (0, 0)),
                  pl.BlockSpec((d, blk), lambda j: (0, j)),
                  pl.BlockSpec((1, blk), lambda j: (0, j))],
        out_specs=pl.BlockSpec((n, blk), lambda j: (0, j)),
        out_shape=jax.ShapeDtypeStruct((n, n_out), F32),
        compiler_params=_cparams(("arbitrary",)),
        name="mod",
    )(c, w_ada, b_ada.reshape(1, n_out))


def _zoh_kernel(ldt_ref, are_ref, aim_ref, bre_ref, bim_ref,
                abre_ref, abim_ref, bbre_ref, bbim_ref):
    dt = jnp.exp(ldt_ref[...])
    a_re = are_ref[...]
    a_im = aim_ref[...]
    mag = jnp.exp(a_re * dt)
    ang = a_im * dt
    abar_re = mag * jnp.cos(ang)
    abar_im = mag * jnp.sin(ang)
    den = a_re * a_re + a_im * a_im
    n_re = abar_re - 1.0
    n_im = abar_im
    q_re = (n_re * a_re + n_im * a_im) / den
    q_im = (n_im * a_re - n_re * a_im) / den
    b_re = bre_ref[...]
    b_im = bim_ref[...]
    abre_ref[...] = abar_re
    abim_ref[...] = abar_im
    bbre_ref[...] = q_re * b_re - q_im * b_im
    bbim_ref[...] = q_re * b_im + q_im * b_re


def _zoh(log_dt, a_re, a_im, b_re, b_im):
    g, n, p = b_re.shape
    rows = g * n
    ldt = jnp.broadcast_to(log_dt[:, None], (g, n)).reshape(rows, 1)
    outs = pl.pallas_call(
        _zoh_kernel,
        out_shape=[jax.ShapeDtypeStruct((rows, 1), F32), jax.ShapeDtypeStruct((rows, 1), F32),
                   jax.ShapeDtypeStruct((rows, p), F32), jax.ShapeDtypeStruct((rows, p), F32)],
        name="zoh",
    )(ldt, a_re.reshape(rows, 1), a_im.reshape(rows, 1), b_re.reshape(rows, p), b_im.reshape(rows, p))
    abar_re, abar_im, bbar_re, bbar_im = outs
    return (abar_re.reshape(g, n), abar_im.reshape(g, n),
            bbar_re.reshape(g, n, p), bbar_im.reshape(g, n, p))


def _state_layout(x_re, x_im):
    lead = x_re.shape[:-2]
    hg = SSM_GROUPS // 2
    r = x_re.reshape(lead + (2, 1, hg * SSM_STATE))
    i = x_im.reshape(lead + (2, 1, hg * SSM_STATE))
    return jnp.concatenate([r, i], axis=-2).reshape(lead + (STATE_LANES,))


def _state_unlayout(h):
    lead = h.shape[:-1]
    hg = SSM_GROUPS // 2
    x = h.reshape(lead + (2, 2, hg, SSM_STATE))
    re = x[..., :, 0, :, :].reshape(lead + (SSM_GROUPS, SSM_STATE))
    im = x[..., :, 1, :, :].reshape(lead + (SSM_GROUPS, SSM_STATE))
    return re, im


def _block_diag_b(bbar_re, bbar_im):
    hg = SSM_GROUPS // 2
    rows, cols = hg * SSM_GROUP, hg * SSM_STATE
    own = (jnp.arange(rows) // SSM_GROUP)[:, None] == (jnp.arange(cols) // SSM_STATE)[None, :]
    out = []
    for hf in range(2):
        parts = []
        for bb in (bbar_re, bbar_im):
            b = jnp.transpose(bb[hf * hg:(hf + 1) * hg], (0, 2, 1)).reshape(rows, SSM_STATE)
            parts.append(jnp.where(own, jnp.tile(b, (1, hg)), 0.0))
        out.append(jnp.concatenate(parts, axis=1))
    return jnp.stack(out).astype(BF16)


def _block_diag_c(c_re, c_im):
    hg = SSM_GROUPS // 2
    rows, cols = hg * SSM_STATE, hg * SSM_GROUP
    own = (jnp.arange(rows) // SSM_STATE)[:, None] == (jnp.arange(cols) // SSM_GROUP)[None, :]
    out = []
    for hf in range(2):
        parts = []
        for cc in (c_re, -c_im):
            c = jnp.transpose(cc[hf * hg:(hf + 1) * hg], (0, 2, 1)).reshape(rows, SSM_GROUP)
            parts.append(jnp.where(own, jnp.tile(c, (1, hg)), 0.0))
        out.append(jnp.concatenate(parts, axis=0))
    return jnp.stack(out).astype(BF16)


def _inproj_kernel(x_ref, mod_ref, g_ref, wm_ref, wf_ref, bf_ref, gq_ref, gk_ref, e_ref, tri_ref, cinit_ref,
                   u_ref, zs_ref, q_ref, kout_ref, vout_ref, katt_ref, vatt_ref, za_ref, cum_ref, logf_ref,
                   *rest, rows):
    if rows:
        cum_rows_ref, carry_ref = rest
    else:
        carry_ref, = rest

    @pl.when(pl.program_id(1) == 0)
    def _():
        carry_ref[...] = cinit_ref[0]

    x = x_ref[0]
    ms = jnp.mean(x * x, axis=-1, keepdims=True)
    xn = x * lax.rsqrt(ms + NORM_EPS) * g_ref[...]
    shift = mod_ref[0, 0:1, :]
    scale = mod_ref[0, 1:2, :]
    hb = (xn * (1.0 + scale) + shift).astype(BF16)

    def sec(i):
        return jnp.dot(hb, wm_ref[:, i * 512:(i + 1) * 512], preferred_element_type=F32)

    def head_rmsnorm(y, g):
        y2 = (y * y).astype(BF16)
        ew = e_ref.shape[0]
        msq = jnp.concatenate([jnp.dot(y2[:, c:c + ew], e_ref[...], preferred_element_type=F32)
                               for c in range(0, y.shape[1], ew)], axis=-1)
        return y * lax.rsqrt(msq + NORM_EPS) * g

    u_ref[0] = sec(0)
    zs_ref[0] = sec(1).astype(BF16)
    q = head_rmsnorm(sec(2), gq_ref[...])
    q_ref[0] = (q * (ATTN_SCALE * LOG2E)).astype(BF16)
    k = head_rmsnorm(sec(3), gk_ref[...])
    kout_ref[0] = k
    v = sec(4)
    vout_ref[0] = v
    kb = k.astype(BF16)
    vb = v.astype(BF16)
    for h in range(N_HEADS):
        katt_ref[0, h] = kb[:, h * HEAD_DIM:(h + 1) * HEAD_DIM]
        vatt_ref[0, h] = vb[:, h * HEAD_DIM:(h + 1) * HEAD_DIM]
    za_ref[0] = sec(5).astype(BF16)
    f = jnp.dot(hb, wf_ref[...], preferred_element_type=F32) + bf_ref[...]
    logf = jnp.minimum(f, 0.0) - jnp.log1p(jnp.exp(-jnp.abs(f)))
    if rows:
        logf_ref[0] = logf.T[:N_HEADS, :]
    else:
        logf_ref[0] = logf[:, :N_HEADS]
    lane = lax.broadcasted_iota(jnp.int32, logf.shape, 1)
    hi, mid, lo = (part.astype(F32) for part in _split3(jnp.where(lane < N_HEADS, logf, 0.0)))
    packed = hi + pltpu.roll(mid, N_HEADS, 1) + pltpu.roll(lo, 2 * N_HEADS, 1)
    th = tri_ref.shape[0]
    packed = packed.astype(BF16)
    chunks, above = [], jnp.zeros((1, LANES), F32)
    for r0 in range(0, packed.shape[0], th):
        chunks.append(jnp.dot(tri_ref[...], packed[r0:r0 + th], preferred_element_type=F32) + above)
        above = above + jnp.sum(packed[r0:r0 + th].astype(F32), axis=0, keepdims=True)
    r = jnp.concatenate(chunks, axis=0)
    c = carry_ref[...] + r + pltpu.roll(r, LANES - N_HEADS, 1) + pltpu.roll(r, LANES - 2 * N_HEADS, 1)
    cum_ref[0] = c[:, :N_HEADS]
    if rows:
        cum_rows_ref[0] = c.T[:N_HEADS, :]
    tm = c.shape[0]
    carry_ref[...] = c[tm - 1:tm, :]


def _inproj(x, mod3, norm_g, w_main, w_f, b_f, gq, gk, e_mat, cum_init, tm):
    b, t, d = x.shape
    nt = t // tm
    tok = lambda w, dt: jax.ShapeDtypeStruct((b, t, w), dt)
    heads = jax.ShapeDtypeStruct((b, N_HEADS, t, HEAD_DIM), BF16)
    const = lambda shape: pl.BlockSpec(shape, lambda i, j: (0,) * len(shape))
    tokspec = lambda w: pl.BlockSpec((1, tm, w), lambda i, j: (i, j, 0))
    headspec = pl.BlockSpec((1, N_HEADS, tm, HEAD_DIM), lambda i, j: (i, 0, j, 0))
    rows = tm % LANES == 0
    th = MXU_WIDTH if tm % MXU_WIDTH == 0 else tm
    rowspec = pl.BlockSpec((1, N_HEADS, tm), lambda i, j: (i, 0, j))
    rowshape = jax.ShapeDtypeStruct((b, N_HEADS, t), F32)
    out_specs = [tokspec(SSM_WIDTH), tokspec(SSM_WIDTH), tokspec(ATTN_WIDTH), tokspec(ATTN_WIDTH),
                 tokspec(ATTN_WIDTH), headspec, headspec, tokspec(ATTN_WIDTH), tokspec(N_HEADS)]
    out_shape = [tok(SSM_WIDTH, F32), tok(SSM_WIDTH, BF16), tok(ATTN_WIDTH, BF16), tok(ATTN_WIDTH, F32),
                 tok(ATTN_WIDTH, F32), heads, heads, tok(ATTN_WIDTH, BF16), tok(N_HEADS, F32)]
    out_specs += [rowspec, rowspec] if rows else [tokspec(N_HEADS)]
    out_shape += [rowshape, rowshape] if rows else [tok(N_HEADS, F32)]
    outs = pl.pallas_call(
        functools.partial(_inproj_kernel, rows=rows),
        grid=(b, nt),
        in_specs=[tokspec(d),
                  pl.BlockSpec((1, 3, d), lambda i, j: (i, 0, 0)),
                  const((1, d)), const(w_main.shape), const(w_f.shape), const((1, LANES)),
                  const((1, ATTN_WIDTH)), const((1, ATTN_WIDTH)), const(e_mat.shape), const((th, th)),
                  pl.BlockSpec((1, 1, LANES), lambda i, j: (i, 0, 0))],
        out_specs=out_specs,
        out_shape=out_shape,
        scratch_shapes=[pltpu.VMEM((1, LANES), F32)],
        compiler_params=_cparams(("arbitrary", "arbitrary")),
        name="inproj",
    )(x, mod3, norm_g, w_main, w_f, b_f, gq, gk, e_mat, _tri(th), cum_init)
    main, cum = outs[:8], outs[8]
    if rows:
        logf_rows, cum_rows = outs[9:]
        logf = jnp.transpose(logf_rows, (0, 2, 1))
    else:
        logf = outs[9]
        cum_rows = jnp.transpose(cum, (0, 2, 1))
    return (*main, logf, cum, cum_rows)


def _cumsum_rows_kernel(x_ref, triu_ref, o_ref):
    tc = triu_ref.shape[0]
    total = jnp.zeros((x_ref.shape[1], 1), F32)
    for c0 in range(0, x_ref.shape[2], tc):
        c = total
        for part in _split3(x_ref[0, :, c0:c0 + tc]):
            c = c + jnp.dot(part, triu_ref[...], preferred_element_type=F32)
        o_ref[0, :, c0:c0 + tc] = c
        total = c[:, tc - 1:tc]


def _cumsum_rows(x, tc):
    b, h, t = x.shape
    return pl.pallas_call(
        _cumsum_rows_kernel,
        grid=(b,),
        in_specs=[pl.BlockSpec((1, h, t), lambda i: (i, 0, 0)),
                  pl.BlockSpec((tc, tc), lambda i: (0, 0))],
        out_specs=pl.BlockSpec((1, h, t), lambda i: (i, 0, 0)),
        out_shape=jax.ShapeDtypeStruct((b, h, t), F32),
        compiler_params=_cparams(("arbitrary",)),
        name="cumsum",
    )(x, jnp.triu(jnp.ones((tc, tc), F32)).astype(BF16))


def _s5_kernel(u_ref, h0_ref, are_ref, aim_ref, bblk_ref, cblk_ref, d_ref,
               y_ref, hout_ref,
               ut_ref, utm_ref, bu_ref, ytm_ref, *, tt, pitch):
    nb = u_ref.shape[0]
    nslab = SSM_WIDTH // LANES

    @pl.when(pl.program_id(0) == 0)
    def _():
        hout_ref[...] = h0_ref[...]

    for b in range(nb):
        for k in range(nslab):
            ut_ref[k, b * pitch:b * pitch + tt, :] = u_ref[b, :, k * LANES:(k + 1) * LANES]

    nph = S5_PHASES if tt % (S5_PHASES * SUBLANES) == 0 else 1
    tp = tt // nph
    chunk = 4 * LANES
    quarter = HALF_LANES // 2
    chunks = [(hf * HALF_LANES + c * chunk, hf * HALF_LANES + c * chunk + quarter)
              for hf in range(2) for c in range(quarter // chunk)]

    def b_proj(ph):
        for t in range(ph * tp, (ph + 1) * tp):
            for k in range(nslab):
                utm_ref[t * nb:(t + 1) * nb, k * LANES:(k + 1) * LANES] = ut_ref[k, pl.ds(t, nb, stride=pitch), :]
        rows = slice(ph * tp * nb, (ph + 1) * tp * nb)
        for hf in range(2):
            uh = utm_ref[rows, hf * HALF_CH:(hf + 1) * HALF_CH].astype(BF16)
            bu_ref[rows, hf * HALF_LANES:(hf + 1) * HALF_LANES] = jnp.dot(
                uh, bblk_ref[hf], preferred_element_type=F32)

    def scan(ph):
        state = [(hout_ref[:, lo_re:lo_re + chunk], hout_ref[:, lo_im:lo_im + chunk]) for lo_re, lo_im in chunks]
        for t in range(ph * tp, (ph + 1) * tp):
            r = slice(t * nb, (t + 1) * nb)
            for ci, (lo_re, lo_im) in enumerate(chunks):
                xr, xi = state[ci]
                ar = are_ref[:, lo_re:lo_re + chunk]
                ai = aim_ref[:, lo_re:lo_re + chunk]
                nr = ar * xr - ai * xi + bu_ref[r, lo_re:lo_re + chunk]
                ni = ar * xi + ai * xr + bu_ref[r, lo_im:lo_im + chunk]
                bu_ref[r, lo_re:lo_re + chunk] = nr
                bu_ref[r, lo_im:lo_im + chunk] = ni
                state[ci] = (nr, ni)
        for (lo_re, lo_im), (xr, xi) in zip(chunks, state):
            hout_ref[:, lo_re:lo_re + chunk] = xr
            hout_ref[:, lo_im:lo_im + chunk] = xi

    def c_proj(ph):
        rows = slice(ph * tp * nb, (ph + 1) * tp * nb)
        for hf in range(2):
            xh = bu_ref[rows, hf * HALF_LANES:(hf + 1) * HALF_LANES].astype(BF16)
            yh = jnp.dot(xh, cblk_ref[hf], preferred_element_type=F32)
            yh = yh + d_ref[:, hf * HALF_CH:(hf + 1) * HALF_CH] * utm_ref[rows, hf * HALF_CH:(hf + 1) * HALF_CH]
            for kk in range(HALF_CH // LANES):
                ytm_ref[hf * (HALF_CH // LANES) + kk, rows, :] = yh[:, kk * LANES:(kk + 1) * LANES]
        for b in range(nb):
            for k in range(nslab):
                y_ref[b, ph * tp:(ph + 1) * tp, k * LANES:(k + 1) * LANES] = (
                    ytm_ref[k, pl.ds(ph * tp * nb + b, tp, stride=nb), :]).astype(y_ref.dtype)

    b_proj(0)
    for ph in range(nph):
        if ph + 1 < nph:
            b_proj(ph + 1)
        scan(ph)
        c_proj(ph)


def _s5(u, h0, are_b, aim_b, bblk, cblk, d_row, tt):
    b, t, w = u.shape
    assert b == SUBLANES and t % tt == 0 and tt % SUBLANES == 0
    pitch = tt + SUBLANES
    rows = tt * b
    const = lambda shape: pl.BlockSpec(shape, lambda j: (0,) * len(shape))
    return pl.pallas_call(
        functools.partial(_s5_kernel, tt=tt, pitch=pitch),
        grid=(t // tt,),
        in_specs=[pl.BlockSpec((b, tt, w), lambda j: (0, j, 0)),
                  const((b, STATE_LANES)), const((b, STATE_LANES)), const((b, STATE_LANES)),
                  const(bblk.shape), const(cblk.shape), const((1, w))],
        out_specs=[pl.BlockSpec((b, tt, w), lambda j: (0, j, 0)), const((b, STATE_LANES))],
        out_shape=[jax.ShapeDtypeStruct((b, t, w), BF16), jax.ShapeDtypeStruct((b, STATE_LANES), F32)],
        scratch_shapes=[pltpu.VMEM((w // LANES, b * pitch, LANES), F32),
                        pltpu.VMEM((rows, w), F32),
                        pltpu.VMEM((rows, STATE_LANES), F32),
                        pltpu.VMEM((w // LANES, rows, LANES), F32)],
        compiler_params=_cparams(("arbitrary",)),
        name="s5",
    )(u, h0, are_b, aim_b, bblk, cblk, d_row)


def _attn_kernel(*refs, bq, has_past):
    if has_past:
        (q_ref, kc_ref, vc_ref, cq_ref, ckc_ref, cec_ref, gq_ref, gk_ref, za_ref,
         kp_ref, vp_ref, ckp_ref, cep_ref, o_ref) = refs
    else:
        q_ref, kc_ref, vc_ref, cq_ref, ckc_ref, cec_ref, gq_ref, gk_ref, za_ref, o_ref = refs
    hp = pl.program_id(1)
    heads_per_step = q_ref.shape[2] // HEAD_DIM
    heads = range(heads_per_step)
    nq = q_ref.shape[1] // bq

    smax = (jnp.max(jnp.abs(gq_ref[...]), axis=-1, keepdims=True)
            * jnp.max(jnp.abs(gk_ref[...]), axis=-1, keepdims=True)
            * (HEAD_DIM * ATTN_SCALE * BF16_NORM_SLACK * BF16_NORM_SLACK))
    skip_below = -(SKIP_LOGIT_GAP + 2.0 * smax)
    smax2 = smax * LOG2E

    def load_cur(ref, hh, start, n):
        return ref[0, hh, pl.ds(start, n), :]

    def load_past(ref, hh, start, n):
        return ref[0, pl.ds(start, n), hh * HEAD_DIM:(hh + 1) * HEAD_DIM].astype(BF16)

    cur = (load_cur, kc_ref, vc_ref, ckc_ref)
    past = (load_past, kp_ref, vp_ref, ckp_ref) if has_past else None

    def q_block(qi, online):
        row0 = pl.multiple_of(qi * bq, bq)
        qrows = pl.ds(row0, bq)
        cq_all = cq_ref[0, qrows, :]
        head_lane = lax.broadcasted_iota(jnp.int32, cq_all.shape, 1)
        qs = [q_ref[0, qrows, hh * HEAD_DIM:(hh + 1) * HEAD_DIM] for hh in heads]

        def cq_of(hh, rows):
            h = hp * heads_per_step + hh
            return jnp.sum(jnp.where(head_lane[rows] == h, cq_all[rows], 0.0), axis=-1, keepdims=True)

        def first_live(ce_ref, n_visible):
            dead = None
            for hh in heads:
                cq_first = cq_of(hh, slice(0, 1))
                ce = ce_ref[0, hh]
                idx = lax.broadcasted_iota(jnp.int32, ce.shape, 1)
                d = jnp.logical_and(cq_first - ce < skip_below, idx < n_visible)
                dead = d if dead is None else jnp.logical_and(dead, d)
            return jnp.sum(dead.astype(jnp.int32))

        def finish(outs):
            o = jnp.concatenate(outs, axis=-1)
            o_ref[0, qrows, :] = (o * _silu(za_ref[0, qrows, :].astype(F32))).astype(BF16)

        def sweep(block, src, lo, hi, carry):
            odd = lo + jnp.bitwise_and(hi - lo, 1)
            carry = lax.fori_loop(lo, odd, lambda j, c: block(src, j, c), carry)

            def pair(i, c):
                j = odd + 2 * i
                return block(src, j + 1, block(src, j, c))
            return lax.fori_loop(0, lax.shift_right_logical(hi - odd, 1), pair, carry)

        def all_blocks(block, carry):
            if has_past:
                npb = ckp_ref.shape[2]
                carry = sweep(block, past, first_live(cep_ref, npb), npb, carry)
            return sweep(block, cur, first_live(cec_ref, qi), qi, carry)

        if not online:
            refs_row = [smax2 - cq_of(hh, slice(None)) * LOG2E for hh in heads]

            def piece(src, hh, j, off, sub, causal, carry_h):
                load, k_ref, v_ref, ck_ref = src
                lp, acc = carry_h
                bk = ck_ref.shape[4]
                start = pl.multiple_of(j * bk + off, sub)
                s = lax.dot_general(qs[hh], load(k_ref, hh, start, sub), (((1,), (1,)), ((), ())),
                                    preferred_element_type=F32)
                t = s - ck_ref[0, hh, j][:, off:off + sub] * LOG2E - refs_row[hh]
                if causal:
                    r = lax.broadcasted_iota(jnp.int32, t.shape, 0)
                    c = lax.broadcasted_iota(jnp.int32, t.shape, 1) + off
                    t = jnp.where(r >= c, t, NEG_BIG)
                p = jnp.exp2(t)
                if sub < LANES:
                    psum = jnp.concatenate([p, jnp.zeros((p.shape[0], LANES - sub), F32)], axis=-1)
                else:
                    psum = p[:, 0:LANES]
                    for c0 in range(LANES, sub, LANES):
                        psum = psum + p[:, c0:c0 + LANES]
                pv = jnp.dot(p.astype(BF16), load(v_ref, hh, start, sub), preferred_element_type=F32)
                return lp + psum, acc + pv

            def block(src, j, carry):
                bk = src[3].shape[4]
                sub = min(bk, ATTN_SUB)
                new = []
                for hh in heads:
                    c = carry[hh]
                    for off in range(0, bk, sub):
                        c = piece(src, hh, j, off, sub, False, c)
                    new.append(c)
                return tuple(new)

            carry = tuple((jnp.zeros((bq, LANES), F32), jnp.zeros((bq, HEAD_DIM), F32)) for _ in heads)
            carry = all_blocks(block, carry)
            outs = []
            for hh in heads:
                lp, acc = piece(cur, hh, qi, 0, bq, True, carry[hh])
                outs.append(acc / jnp.sum(lp, axis=-1, keepdims=True))
            finish(outs)
        else:
            def update(s, v, carry_h):
                m, l, acc = carry_h
                m_new = jnp.maximum(m, jnp.max(s, axis=-1, keepdims=True))
                alpha = jnp.exp2(m - m_new)
                p = jnp.exp2(s - m_new)
                l = alpha * l + jnp.sum(p, axis=-1, keepdims=True)
                acc = alpha * acc + jnp.dot(p.astype(BF16), v, preferred_element_type=F32)
                return m_new, l, acc

            def scores(src, hh, j):
                load, k_ref, _, ck_ref = src
                bk = ck_ref.shape[4]
                start = pl.multiple_of(j * bk, bk)
                s = lax.dot_general(qs[hh], load(k_ref, hh, start, bk), (((1,), (1,)), ((), ())),
                                    preferred_element_type=F32)
                return s - ck_ref[0, hh, j] * LOG2E, start, bk

            def block(src, j, carry):
                new = []
                for hh in heads:
                    s, start, bk = scores(src, hh, j)
                    new.append(update(s, src[0](src[2], hh, start, bk), carry[hh]))
                return tuple(new)

            carry = tuple((jnp.full((bq, 1), NEG_BIG, F32), jnp.zeros((bq, 1), F32),
                           jnp.zeros((bq, HEAD_DIM), F32)) for _ in heads)
            carry = all_blocks(block, carry)
            row = lax.broadcasted_iota(jnp.int32, (bq, bq), 0)
            col = lax.broadcasted_iota(jnp.int32, (bq, bq), 1)
            outs = []
            for hh in heads:
                s, start, bk = scores(cur, hh, qi)
                s = jnp.where(row >= col, s, NEG_BIG)
                _, l, acc = update(s, load_cur(vc_ref, hh, start, bk), carry[hh])
                outs.append(acc / l)
            finish(outs)
        return 0

    small_logits = jnp.max(smax2) * 2.0 <= BOUNDED_LOGIT_RANGE

    @pl.when(small_logits)
    def _():
        lax.fori_loop(0, nq, lambda qi, _: q_block(qi, False), 0)

    @pl.when(jnp.logical_not(small_logits))
    def _():
        lax.fori_loop(0, nq, lambda qi, _: q_block(qi, True), 0)


def _row_blocks(cum_rows, blk):
    b, h, t = cum_rows.shape
    rows = cum_rows.reshape(b, h, t // blk, 1, blk)
    return rows, rows[:, :, :, 0, blk - 1].reshape(b, h, 1, t // blk)


def _attn(q, k_cur, v_cur, cum, cum_rows, za, gq, gk, bq, past=None):
    b, tq, _ = q.shape
    hps = LANES // HEAD_DIM
    nq = tq // bq
    ckc, cec = _row_blocks(cum_rows, bq)
    qspec = pl.BlockSpec((1, tq, LANES), lambda i, h: (i, 0, h))
    kvspec = pl.BlockSpec((1, hps, tq, HEAD_DIM), lambda i, h: (i, h, 0, 0))
    rowspec = lambda n, w: pl.BlockSpec((1, hps, n, 1, w), lambda i, h: (i, h, 0, 0, 0))
    endspec = lambda n: pl.BlockSpec((1, hps, 1, n), lambda i, h: (i, h, 0, 0))
    gspec = pl.BlockSpec((1, ATTN_WIDTH), lambda i, h: (0, 0))
    in_specs = [qspec, kvspec, kvspec, pl.BlockSpec((1, tq, N_HEADS), lambda i, h: (i, 0, 0)),
                rowspec(nq, bq), endspec(nq), gspec, gspec, qspec]
    args = [q, k_cur, v_cur, cum, ckc, cec, gq, gk, za]
    if past is not None:
        k_past, v_past, cum_past_rows, bkp = past
        tp = k_past.shape[1]
        ckp, cep = _row_blocks(cum_past_rows, bkp)
        pastspec = pl.BlockSpec((1, tp, LANES), lambda i, h: (i, 0, h))
        in_specs += [pastspec, pastspec, rowspec(tp // bkp, bkp), endspec(tp // bkp)]
        args += [k_past, v_past, ckp, cep]
    return pl.pallas_call(
        functools.partial(_attn_kernel, bq=bq, has_past=past is not None),
        grid=(b, N_HEADS // hps),
        in_specs=in_specs,
        out_specs=qspec,
        out_shape=jax.ShapeDtypeStruct((b, tq, ATTN_WIDTH), BF16),
        compiler_params=_cparams(("arbitrary", "arbitrary")),
        name="attn",
    )(*args)


def _out_kernel(x_ref, mod_ref, ys_ref, zs_ref, ya_ref, wglu_ref, bglu_ref, wo_ref, o_ref):
    y = ys_ref[0].astype(F32)
    g = y * (0.5 * (1.0 + jnp.tanh(GELU_C * (y + 0.044715 * (y * y * y)))))
    glu = jnp.dot(g.astype(BF16), wglu_ref[...], preferred_element_type=F32) + bglu_ref[...]
    y1 = g * _sigmoid(glu) * _silu(zs_ref[0].astype(F32))
    mixed = (jnp.dot(y1.astype(BF16), wo_ref[0:SSM_WIDTH, :], preferred_element_type=F32)
             + jnp.dot(ya_ref[0], wo_ref[SSM_WIDTH:, :], preferred_element_type=F32))
    o_ref[0] = x_ref[0] + mod_ref[0, 2:3, :] * mixed


def _out(x, mod3, y_ssm, zs, y_att, w_glu, b_glu, w_out, tm):
    b, t, d = x.shape
    const = lambda shape: pl.BlockSpec(shape, lambda i, j: (0,) * len(shape))
    tokspec = lambda w: pl.BlockSpec((1, tm, w), lambda i, j: (i, j, 0))
    return pl.pallas_call(
        _out_kernel,
        grid=(b, t // tm),
        in_specs=[tokspec(d), pl.BlockSpec((1, 3, d), lambda i, j: (i, 0, 0)),
                  tokspec(SSM_WIDTH), tokspec(SSM_WIDTH), tokspec(ATTN_WIDTH),
                  const(w_glu.shape), const((1, SSM_WIDTH)), const(w_out.shape)],
        out_specs=tokspec(d),
        out_shape=jax.ShapeDtypeStruct((b, t, d), F32),
        compiler_params=_cparams(("arbitrary", "arbitrary")),
        name="out",
    )(x, mod3, y_ssm, zs, y_att, w_glu, b_glu, w_out)


def _tile(t, pref):
    return pref if t % pref == 0 else t


def _layer(x, mod3, h0, past, p):
    b, t, d = x.shape
    tm = _tile(t, 512)
    if past is None:
        cum_init = jnp.zeros((b, 1, LANES), F32)
        attn_past = None
    else:
        ck_past, cv_past, clogf = past
        tp = ck_past.shape[1]
        cum_past_rows = _cumsum_rows(jnp.transpose(clogf.astype(F32), (0, 2, 1)), _tile(tp, 512))
        cum_init = jnp.pad(cum_past_rows[:, :, tp - 1], ((0, 0), (0, LANES - N_HEADS))).reshape(b, 1, LANES)
        attn_past = (ck_past.reshape(b, tp, ATTN_WIDTH), cv_past.reshape(b, tp, ATTN_WIDTH),
                     cum_past_rows, _tile(tp, 512))

    u, zs, q, k_out, v_out, k_att, v_att, za, logf, cum, cum_rows = _inproj(
        x, mod3, p['norm_g'], p['w_main'], p['w_f'], p['b_f'], p['gq'], p['gk'], p['e_mat'], cum_init, tm)
    y_ssm, h_new = _s5(u, h0, p['are_b'], p['aim_b'], p['bblk'], p['cblk'], p['d_row'], _tile(t, 128))
    y_att = _attn(q, k_att, v_att, cum, cum_rows, za, p['gq'], p['gk'], _tile(t, 512), attn_past)
    y = _out(x, mod3, y_ssm, zs, y_att, p['w_glu'], p['b_glu'], p['w_out'], _tile(t, 2048))
    return y, k_out, v_out, logf, h_new


def kernel(x_prompt, x_sample, cache_k, cache_v, cache_logf, state_ssm_re, state_ssm_im, c_prompt, c_sample,
           w_ada, b_ada, norm_g, w_in, b_f, q_norm_g, k_norm_g, ssm_log_dt, ssm_a_re, ssm_a_im,
           ssm_b_re, ssm_b_im, ssm_c_re, ssm_c_im, ssm_d, w_glu, b_glu, w_out):
    depth = w_ada.shape[0]
    d = x_prompt.shape[-1]
    bp = x_prompt.shape[0]
    bs = x_sample.shape[0]
    xp, xs = x_prompt, x_sample
    outs_p, outs_s = [], []
    for l in range(depth):
        mod = _mod(jnp.concatenate([c_prompt, c_sample], axis=0), w_ada[l], b_ada[l])
        mod3 = mod.reshape(bp + bs, 3, d)
        abar_re, abar_im, bbar_re, bbar_im = _zoh(ssm_log_dt[l], ssm_a_re[l], ssm_a_im[l],
                                                  ssm_b_re[l], ssm_b_im[l])
        a_lanes_re = _state_layout(abar_re, abar_re)
        a_lanes_im = _state_layout(abar_im, abar_im)
        split = 2 * SSM_WIDTH + 4 * ATTN_WIDTH
        hd = jnp.arange(MXU_WIDTH) // HEAD_DIM
        p = dict(
            norm_g=norm_g[l].reshape(1, d),
            w_main=w_in[l][:, :split].astype(BF16),
            w_f=jnp.pad(w_in[l][:, split:], ((0, 0), (0, LANES - N_HEADS))).astype(BF16),
            b_f=jnp.pad(b_f[l], (0, LANES - N_HEADS)).reshape(1, LANES),
            gq=jnp.tile(q_norm_g[l], N_HEADS).reshape(1, ATTN_WIDTH),
            gk=jnp.tile(k_norm_g[l], N_HEADS).reshape(1, ATTN_WIDTH),
            e_mat=(hd[:, None] == hd[None, :]).astype(BF16) * (1.0 / HEAD_DIM),
            are_b=jnp.broadcast_to(a_lanes_re[None], (SUBLANES, STATE_LANES)),
            aim_b=jnp.broadcast_to(a_lanes_im[None], (SUBLANES, STATE_LANES)),
            bblk=_block_diag_b(bbar_re, bbar_im),
            cblk=_block_diag_c(ssm_c_re[l], ssm_c_im[l]),
            d_row=ssm_d[l].reshape(1, SSM_WIDTH),
            w_glu=w_glu[l].astype(BF16), b_glu=b_glu[l].reshape(1, SSM_WIDTH),
            w_out=w_out[l].astype(BF16),
        )
        h0_p = jnp.zeros((bp, STATE_LANES), F32)
        xp, k1, v1, f1, h1 = _layer(xp, mod3[:bp], h0_p, None, p)
        h0_s = _state_layout(state_ssm_re[l], state_ssm_im[l])
        xs, k2, v2, f2, h2 = _layer(xs, mod3[bp:], h0_s, (cache_k[l], cache_v[l], cache_logf[l]), p)
        outs_p.append((k1, v1, f1) + _state_unlayout(h1))
        outs_s.append((k2, v2, f2) + _state_unlayout(h2))

    def stack(items, i, shape_tail):
        arr = jnp.stack([it[i] for it in items])
        return arr.reshape(arr.shape[:3] + shape_tail) if shape_tail else arr

    hd_tail = (N_HEADS, HEAD_DIM)
    return (xp, xs,
            stack(outs_p, 0, hd_tail), stack(outs_p, 1, hd_tail), stack(outs_p, 2, ()),
            stack(outs_p, 3, ()), stack(outs_p, 4, ()),
            stack(outs_s, 0, hd_tail), stack(outs_s, 1, hd_tail), stack(outs_s, 2, ()),
            stack(outs_s, 3, ()), stack(outs_s, 4, ()))
```
